```python
import math
import jax
import jax.numpy as jnp
from jax import lax
import numpy as np

D_MODEL = 4096
BATCH = 32
SEQ = 256
DEPTH = 2
DEC_BATCH = 2
DEC_SEQ = 1024
PAST_LEN = 256

GRID_W = 64
N_DIR = 2
EPS = 1e-6
CONV_W = 4
DN_HEADS = 16
DN_DK = 128
DN_DV = 128
DN_CHUNK = 64
QK_W = DN_HEADS * DN_DK
V_W = DN_HEADS * DN_DV
LRU_WIDTH = 2048
LRU_BLOCKS = 16
LRU_BW = LRU_WIDTH // LRU_BLOCKS
LRU_C = 8.0
N_EXPERTS = 64
D_EXPERT = 512
TOP_K = 6
N_GROUPS = 8
TOPK_GROUPS = 4
D_SHARED = 1024
ROUTED_SCALE = 2.5
IN_SIZES = (QK_W, QK_W, V_W, V_W, N_DIR * DN_HEADS, N_DIR * DN_HEADS, LRU_WIDTH, LRU_WIDTH, D_MODEL, D_MODEL)
N_IN = sum(IN_SIZES)

kernel_name = "bidir_deltanet_rglru_moe_diffusion_step"


def rms_norm(x, g):
    xf = x.astype(jnp.float32)
    y = xf * lax.rsqrt(jnp.mean(jnp.square(xf), axis=-1, keepdims=True) + EPS)
    return (y * g.astype(jnp.float32)).astype(x.dtype)


def l2_normalize(x):
    return x * lax.rsqrt(jnp.sum(x * x, axis=-1, keepdims=True) + EPS)


def centred_dwconv(x, w):
    k = w.shape[0]
    left = k // 2
    return lax.conv_general_dilated(
        x, w[:, None, :].astype(x.dtype), window_strides=(1,), padding=[(left, k - 1 - left)],
        dimension_numbers=("NWC", "WIO", "NWC"), feature_group_count=x.shape[-1])


def grid_sincos(n_tokens, dim, dtype):
    rows = n_tokens // GRID_W
    quarter = dim // 4
    omega = 1.0 / (10000.0 ** (jnp.arange(quarter, dtype=jnp.float32) / quarter))
    r = jnp.broadcast_to(jnp.arange(rows, dtype=jnp.float32)[:, None, None] * omega, (rows, GRID_W, quarter))
    cl = jnp.broadcast_to(jnp.arange(GRID_W, dtype=jnp.float32)[None, :, None] * omega, (rows, GRID_W, quarter))
    emb = jnp.concatenate([jnp.sin(r), jnp.cos(r), jnp.sin(cl), jnp.cos(cl)], axis=-1)
    return emb.reshape(n_tokens, dim).astype(dtype)


def gated_delta_rule(q, k, v, g, beta, s0):
    b, l, h, _ = k.shape
    dv = v.shape[-1]
    n = l // DN_CHUNK

    def chunks(t):
        t = t.reshape((b, n, DN_CHUNK, h) + t.shape[3:])
        return jnp.moveaxis(t, (1, 3), (0, 2))

    qc, kc, vc, gc, bc = chunks(q), chunks(k), chunks(v), chunks(g), chunks(beta)
    gc = jnp.cumsum(gc, axis=-1)
    idx = jnp.arange(DN_CHUNK)
    incl = idx[:, None] >= idx[None, :]
    strict = idx[:, None] > idx[None, :]
    diff = gc[..., :, None] - gc[..., None, :]
    decay = jnp.where(incl, jnp.exp(jnp.where(incl, diff, 0.0)), 0.0)
    kb = kc * bc[..., None]
    a_mat = jnp.where(strict, jnp.einsum("nbhid,nbhjd->nbhij", kb, kc) * decay, 0.0)
    eye = jnp.eye(DN_CHUNK, dtype=q.dtype)
    t_mat = lax.linalg.triangular_solve(a_mat + eye, jnp.broadcast_to(eye, a_mat.shape),
                                        left_side=True, lower=True, unit_diagonal=True)
    u = t_mat @ (vc * bc[..., None])
    w = t_mat @ (kb * jnp.exp(gc)[..., None])
    qk = jnp.einsum("nbhid,nbhjd->nbhij", qc, kc) * decay
    q_dec = qc * jnp.exp(gc)[..., None]
    g_last = gc[..., -1]
    k_dec = kc * jnp.exp(g_last[..., None] - gc)[..., None]

    def step(s, xs):
        q_i, qk_i, u_i, w_i, k_i, gl_i = xs
        v_new = u_i - w_i @ s
        o = q_i @ s + qk_i @ v_new
        s = s * jnp.exp(gl_i)[..., None, None] + jnp.einsum("bhcd,bhce->bhde", k_i, v_new)
        return s, o

    s_fin, o = lax.scan(step, s0, (q_dec, qk, u, w, k_dec, g_last))
    o = jnp.moveaxis(o, (0, 2), (1, 3)).reshape(b, l, h, dv)
    return o, s_fin


def deltanet_branch(q, k, v, z, beta_raw, a_raw, conv_w, a_log, dt_bias, norm_g, w_branch, s0):
    bsz, l, _ = q.shape
    dtype = q.dtype
    f32 = jnp.float32
    qkv = jax.nn.silu(centred_dwconv(jnp.concatenate([q, k, v], axis=-1), conv_w)).astype(f32)
    qh, kh, vh = jnp.split(qkv, [QK_W, 2 * QK_W], axis=-1)
    qh = l2_normalize(qh.reshape(bsz, l, DN_HEADS, DN_DK)) * (DN_DK ** -0.5)
    kh = l2_normalize(kh.reshape(bsz, l, DN_HEADS, DN_DK))
    vh = vh.reshape(bsz, l, DN_HEADS, DN_DV)
    beta = jax.nn.sigmoid(beta_raw.astype(f32)).reshape(bsz, l, N_DIR, DN_HEADS)
    g = -jnp.exp(a_log.astype(f32)) * jax.nn.softplus(
        a_raw.astype(f32).reshape(bsz, l, N_DIR, DN_HEADS) + dt_bias.astype(f32))
    s0 = s0.astype(f32)
    o_f, s_f = gated_delta_rule(qh, kh, vh, g[:, :, 0], beta[:, :, 0], s0[:, 0])
    flip = lambda t: jnp.flip(t, axis=1)
    o_b, s_b = gated_delta_rule(flip(qh), flip(kh), flip(vh), flip(g[:, :, 1]), flip(beta[:, :, 1]), s0[:, 1])
    o = o_f + flip(o_b)
    o = o * lax.rsqrt(jnp.mean(o * o, axis=-1, keepdims=True) + EPS) * norm_g.astype(f32)
    o = o * jax.nn.silu(z.astype(f32).reshape(bsz, l, DN_HEADS, DN_DV))
    out = o.reshape(bsz, l, V_W).astype(dtype) @ w_branch
    return out, jnp.stack([s_f, s_b], axis=1).astype(dtype)


def linear_scan(a, b, h0):
    b = b.at[:, 0].add(a[:, 0] * h0)

    def combine(lhs, rhs):
        return lhs[0] * rhs[0], rhs[0] * lhs[1] + rhs[1]

    _, h = lax.associative_scan(combine, (a, b), axis=1)
    return h, h[:, -1]


def rglru_branch(x_in, y_in, conv_w, conv_b, wa, ba, wx, bx, lam, w_branch, h0):
    bsz, l, _ = x_in.shape
    dtype = x_in.dtype
    f32 = jnp.float32
    xc = centred_dwconv(x_in, conv_w) + conv_b
    xb = xc.reshape(bsz, l, LRU_BLOCKS, LRU_BW)
    r = jax.nn.sigmoid((jnp.einsum("blni,dnio->bldno", xb, wa).reshape(bsz, l, N_DIR, LRU_WIDTH) + ba).astype(f32))
    i = jax.nn.sigmoid((jnp.einsum("blni,dnio->bldno", xb, wx).reshape(bsz, l, N_DIR, LRU_WIDTH) + bx).astype(f32))
    log_a = -LRU_C * r * jax.nn.softplus(-lam.astype(f32))
    a = jnp.exp(log_a)
    b_in = jnp.sqrt(-jnp.expm1(2.0 * log_a)) * i * xc.astype(f32)[:, :, None, :]
    h0 = h0.astype(f32)
    flip = lambda t: jnp.flip(t, axis=1)
    h_f, s_f = linear_scan(a[:, :, 0], b_in[:, :, 0], h0[:, 0])
    h_b, s_b = linear_scan(flip(a[:, :, 1]), flip(b_in[:, :, 1]), h0[:, 1])
    h = h_f + flip(h_b)
    out = (h * jax.nn.gelu(y_in.astype(f32))).astype(dtype) @ w_branch
    return out, jnp.stack([s_f, s_b], axis=1).astype(dtype)


def swiglu(t, wg, wu, wd):
    return (jax.nn.silu(t @ wg) * (t @ wu)) @ wd


def moe_ffn(h, router_w, router_bias, w_gate, w_up, w_down, ws_gate, ws_up, ws_down):
    bsz, l, d = h.shape
    f32 = jnp.float32
    t = h.reshape(bsz * l, d)
    scores = jax.nn.sigmoid((t @ router_w).astype(f32))
    sel = scores + router_bias.astype(f32)
    grp = sel.reshape(-1, N_GROUPS, N_EXPERTS // N_GROUPS)
    grp_score = jnp.sum(lax.top_k(grp, 2)[0], axis=-1)
    _, grp_idx = lax.top_k(grp_score, TOPK_GROUPS)
    grp_mask = jnp.sum(jax.nn.one_hot(grp_idx, N_GROUPS, dtype=f32), axis=-2)
    exp_mask = jnp.repeat(grp_mask, N_EXPERTS // N_GROUPS, axis=-1) > 0
    _, top_idx = lax.top_k(jnp.where(exp_mask, sel, -jnp.inf), TOP_K)
    top_w = jnp.take_along_axis(scores, top_idx, axis=-1)
    top_w = top_w / jnp.sum(top_w, axis=-1, keepdims=True) * ROUTED_SCALE
    gates = jnp.sum(jax.nn.one_hot(top_idx, N_EXPERTS, dtype=f32) * top_w[..., None], axis=-2)

    def expert_step(acc, xs):
        wg, wu, wd, g_e = xs
        return acc + g_e[:, None].astype(t.dtype) * swiglu(t, wg, wu, wd), None

    routed, _ = lax.scan(expert_step, jnp.zeros_like(t), (w_gate, w_up, w_down, gates.T))
    out = routed + swiglu(t, ws_gate, ws_up, ws_down)
    return out.reshape(bsz, l, d)


def trunk_block(x, mod, lp, delta_s0, lru_s0):
    shift1, scale1, gate1, shift2, scale2, gate2 = jnp.split(mod, 6, axis=-1)
    h = rms_norm(x, lp["norm1_g"]) * (1 + scale1) + shift1
    proj = h @ lp["w_in"]
    q, k, v, z, beta_raw, a_raw, lx, ly, ga, gb = jnp.split(proj, np.cumsum(IN_SIZES)[:-1].tolist(), axis=-1)
    o_a, s_delta = deltanet_branch(q, k, v, z, beta_raw, a_raw, lp["dn_conv_w"], lp["dn_a_log"],
                                   lp["dn_dt_bias"], lp["dn_norm_g"], lp["w_branch_a"], delta_s0)
    o_b, s_lru = rglru_branch(lx, ly, lp["lru_conv_w"], lp["lru_conv_b"], lp["lru_wa"], lp["lru_ba"],
                              lp["lru_wx"], lp["lru_bx"], lp["lru_lambda"], lp["w_branch_b"], lru_s0)
    merged = jax.nn.sigmoid(ga) * o_a + jax.nn.sigmoid(gb) * o_b
    x = x + gate1 * (merged @ lp["w_out"])
    h2 = rms_norm(x, lp["norm2_g"]) * (1 + scale2) + shift2
    x = x + gate2 * moe_ffn(h2, lp["router_w"], lp["router_bias"], lp["w_exp_gate"], lp["w_exp_up"],
                            lp["w_exp_down"], lp["w_sh_gate"], lp["w_sh_up"], lp["w_sh_down"])
    return x, s_delta, s_lru


def setup_inputs(seed: int = 0) -> dict:
    key = jax.random.key(seed)
    ks = iter(jax.random.split(key, 48))

    def nrm(shape, scale):
        return jax.random.normal(next(ks), shape, jnp.float32) * scale

    def uni(shape, lo, hi):
        return jax.random.uniform(next(ks), shape, jnp.float32, lo, hi)

    d = D_MODEL
    a0 = uni((DEPTH, N_DIR, LRU_WIDTH), 0.9, 0.999)
    s = a0 ** (1.0 / LRU_C)
    lru_lambda = jnp.log(s) - jnp.log1p(-s)
    dt = jnp.exp(uni((DEPTH, N_DIR, DN_HEADS), math.log(1e-3), math.log(1e-1)))
    dn_dt_bias = dt + jnp.log(-jnp.expm1(-dt))
    dn_a_log = jnp.log(uni((DEPTH, N_DIR, DN_HEADS), 1.0, 16.0))
    return {
        "x_prompt": nrm((BATCH, SEQ, d), 1.0),
        "x_sample": nrm((DEC_BATCH, DEC_SEQ, d), 1.0),
        "state_delta": nrm((DEC_BATCH, DEPTH, N_DIR, DN_HEADS, DN_DK, DN_DV), 0.3),
        "state_lru": nrm((DEC_BATCH, DEPTH, N_DIR, LRU_WIDTH), 0.5),
        "c": nrm((DEC_BATCH, d), 1.0),
        "c_ctx": nrm((d,), 1.0),
        "w_ada": nrm((DEPTH, d, 6 * d), 0.5 * d ** -0.5),
        "b_ada": nrm((DEPTH, 6 * d), 0.01),
        "norm1_g": 1.0 + nrm((DEPTH, d), 0.02),
        "w_in": nrm((DEPTH, d, N_IN), d ** -0.5),
        "dn_conv_w": nrm((DEPTH, CONV_W, 2 * QK_W + V_W), CONV_W ** -0.5),
        "dn_a_log": dn_a_log,
        "dn_dt_bias": dn_dt_bias,
        "dn_norm_g": 1.0 + nrm((DEPTH, DN_DV), 0.02),
        "w_branch_a": nrm((DEPTH, V_W, d), V_W ** -0.5),
        "lru_conv_w": nrm((DEPTH, CONV_W, LRU_WIDTH), CONV_W ** -0.5),
        "lru_conv_b": nrm((DEPTH, LRU_WIDTH), 0.01),
        "lru_wa": nrm((DEPTH, N_DIR, LRU_BLOCKS, LRU_BW, LRU_BW), LRU_BW ** -0.5),
        "lru_ba": nrm((DEPTH, N_DIR, LRU_WIDTH), 0.01),
        "lru_wx": nrm((DEPTH, N_DIR, LRU_BLOCKS, LRU_BW, LRU_BW), LRU_BW ** -0.5),
        "lru_bx": nrm((DEPTH, N_DIR, LRU_WIDTH), 0.01),
        "lru_lambda": lru_lambda,
        "w_branch_b": nrm((DEPTH, LRU_WIDTH, d), LRU_WIDTH ** -0.5),
        "w_out": nrm((DEPTH, d, d), d ** -0.5),
        "norm2_g": 1.0 + nrm((DEPTH, d), 0.02),
        "router_w": nrm((DEPTH, d, N_EXPERTS), d ** -0.5),
        "router_bias": nrm((DEPTH, N_EXPERTS), 0.01),
        "w_exp_gate": nrm((DEPTH, N_EXPERTS, d, D_EXPERT), d ** -0.5),
        "w_exp_up": nrm((DEPTH, N_EXPERTS, d, D_EXPERT), d ** -0.5),
        "w_exp_down": nrm((DEPTH, N_EXPERTS, D_EXPERT, d), D_EXPERT ** -0.5),
        "w_sh_gate": nrm((DEPTH, d, D_SHARED), d ** -0.5),
        "w_sh_up": nrm((DEPTH, d, D_SHARED), d ** -0.5),
        "w_sh_down": nrm((DEPTH, D_SHARED, d), D_SHARED ** -0.5),
        "final_norm_g": 1.0 + nrm((d,), 0.02),
    }


def reference(x_prompt, x_sample, state_delta, state_lru, c, c_ctx, w_ada, b_ada, norm1_g, w_in,
              dn_conv_w, dn_a_log, dn_dt_bias, dn_norm_g, w_branch_a, lru_conv_w, lru_conv_b,
              lru_wa, lru_ba, lru_wx, lru_bx, lru_lambda, w_branch_b, w_out, norm2_g, router_w,
              router_bias, w_exp_gate, w_exp_up, w_exp_down, w_sh_gate, w_sh_up, w_sh_down,
              final_norm_g):
    dtype = x_prompt.dtype
    n_ctx_req = x_prompt.shape[0]
    xp = x_prompt
    xs = x_sample + grid_sincos(x_sample.shape[1], D_MODEL, x_sample.dtype)[None]
    zero_delta = jnp.zeros((n_ctx_req, N_DIR, DN_HEADS, DN_DK, DN_DV), dtype)
    zero_lru = jnp.zeros((n_ctx_req, N_DIR, LRU_WIDTH), dtype)
    cond_ctx = jax.nn.silu(c_ctx)[None, None, :]
    cond_lat = jax.nn.silu(c)[:, None, :]
    new_delta = []
    new_lru = []
    for l in range(DEPTH):
        lp = {
            "norm1_g": norm1_g[l], "w_in": w_in[l], "dn_conv_w": dn_conv_w[l], "dn_a_log": dn_a_log[l],
            "dn_dt_bias": dn_dt_bias[l], "dn_norm_g": dn_norm_g[l], "w_branch_a": w_branch_a[l],
            "lru_conv_w": lru_conv_w[l], "lru_conv_b": lru_conv_b[l], "lru_wa": lru_wa[l],
            "lru_ba": lru_ba[l], "lru_wx": lru_wx[l], "lru_bx": lru_bx[l], "lru_lambda": lru_lambda[l],
            "w_branch_b": w_branch_b[l], "w_out": w_out[l], "norm2_g": norm2_g[l],
            "router_w": router_w[l], "router_bias": router_bias[l], "w_exp_gate": w_exp_gate[l],
            "w_exp_up": w_exp_up[l], "w_exp_down": w_exp_down[l], "w_sh_gate": w_sh_gate[l],
            "w_sh_up": w_sh_up[l], "w_sh_down": w_sh_down[l],
        }
        mod_ctx = cond_ctx @ w_ada[l] + b_ada[l]
        mod_lat = cond_lat @ w_ada[l] + b_ada[l]
        xp, s_delta, s_lru = trunk_block(xp, mod_ctx, lp, zero_delta, zero_lru)
        new_delta.append(s_delta)
        new_lru.append(s_lru)
        xs, _, _ = trunk_block(xs, mod_lat, lp, state_delta[:, l], state_lru[:, l])
    y_prompt = rms_norm(xp, final_norm_g)
    y_sample = rms_norm(xs, final_norm_g)
    new_state_delta = jnp.stack(new_delta, axis=1)
    new_state_lru = jnp.stack(new_lru, axis=1)
    return (y_prompt, y_sample, new_state_delta, new_state_lru)
```

```python
import functools
import math

import jax
import jax.numpy as jnp
from jax import lax
from jax.experimental import pallas as pl
from jax.experimental.pallas import tpu as pltpu

F32 = jnp.float32
BF16 = jnp.bfloat16
I32 = jnp.int32

EPS = 1e-6
GRID_W = 64
DN_CHUNK = 64
LRU_C = 8.0
TOP_K = 6
N_GROUPS = 8
TOPK_GROUPS = 4
ROUTED_SCALE = 2.5

V7X_VMEM_LIMIT_BYTES = 56 * 1024 * 1024
LANES = 128
ROW_BLK = 256
MOE_TM = 256
MOD_ROWS = 8


def _cparams(*sem):
    return pltpu.CompilerParams(dimension_semantics=sem, vmem_limit_bytes=V7X_VMEM_LIMIT_BYTES)


def _dot(a, b):
    return jnp.dot(a, b, preferred_element_type=F32)


def _dot_nt(a, b):
    return lax.dot_general(a, b, (((1,), (1,)), ((), ())), preferred_element_type=F32)


def _dot_tn(a, b):
    return lax.dot_general(a, b, (((0,), (0,)), ((), ())), preferred_element_type=F32)


def _split(a):
    hi = a.astype(BF16)
    lo = (a - hi.astype(F32)).astype(BF16)
    return hi, lo


def _dot3(a, b, dot=_dot):
    ah, al = _split(a)
    bh, bl = _split(b)
    return dot(ah, bh) + (dot(ah, bl) + dot(al, bh))


def _sigmoid(x):
    return 1.0 / (1.0 + jnp.exp(-x))


def _silu(x):
    return x * _sigmoid(x)


def _softplus(x):
    return jnp.maximum(x, 0.0) + jnp.log1p(jnp.exp(-jnp.abs(x)))


def _gelu_tanh(x):
    return 0.5 * x * (1.0 + jnp.tanh(math.sqrt(2.0 / math.pi) * (x + 0.044715 * (x * x * x))))


def _l2n(x):
    return x * lax.rsqrt(jnp.sum(x * x, axis=-1, keepdims=True) + EPS)


def _rms_mod(x, g, scale, shift):
    ms = jnp.mean(x * x, axis=-1, keepdims=True)
    y = x * lax.rsqrt(ms + EPS) * g
    return y * (1.0 + scale) + shift


def _pick(n, pref):
    c = pref
    while n % c:
        c //= 2
    return c


class Groups:
    def __init__(self, n_p, l_p, n_s, l_s):
        self.n_p, self.l_p, self.n_s, self.l_s = n_p, l_p, n_s, l_s
        self.t_p = n_p * l_p
        self.t = self.t_p + n_s * l_s

    def row_tile(self, pref):
        return _pick(math.gcd(self.t_p, self.l_s), pref)

    def mod_row(self, i, tm):
        start = i * tm
        return jnp.where(start < self.t_p, 0, 1 + (start - self.t_p) // self.l_s)


def _mod_spec(grp, layer, d, k, tm, ncols=None, two_d=False):
    if two_d:
        per = d // ncols
        return pl.BlockSpec((None, 1, ncols),
                            lambda i, j: (layer * MOD_ROWS + grp.mod_row(i, tm), 0, k * per + j))
    return pl.BlockSpec((None, 1, d), lambda i: (layer * MOD_ROWS + grp.mod_row(i, tm), 0, k))


def _ada_kernel(c_ref, w_ref, b_ref, o_ref):
    s = _silu(c_ref[...]).astype(BF16)
    o_ref[...] = _dot(s, w_ref[...].astype(BF16)) + b_ref[...]


def ada_mod(cond, w_ada, b_ada, tn=512):
    depth, d, n = w_ada.shape
    tn = _pick(n, tn)
    return pl.pallas_call(
        _ada_kernel,
        grid=(depth, n // tn),
        in_specs=[
            pl.BlockSpec((MOD_ROWS, d), lambda l, j: (0, 0)),
            pl.BlockSpec((None, d, tn), lambda l, j: (l, 0, j)),
            pl.BlockSpec((None, 1, tn), lambda l, j: (l, 0, j)),
        ],
        out_specs=pl.BlockSpec((None, MOD_ROWS, tn), lambda l, j: (l, 0, j)),
        out_shape=jax.ShapeDtypeStruct((depth, MOD_ROWS, n), F32),
        compiler_params=_cparams("arbitrary", "arbitrary"),
        name="ada_mod",
    )(cond, w_ada, b_ada.reshape(depth, 1, n))


def _norm1_kernel(x_ref, g_ref, sh_ref, sc_ref, wg_ref, h_ref, gates_ref):
    h = _rms_mod(x_ref[...], g_ref[...], sc_ref[...], sh_ref[...])
    h_ref[...] = h.astype(BF16)
    gates_ref[...] = _dot3(h, wg_ref[...])


def norm1_gates(x, g, mod, layer, w_gates, grp, tm=ROW_BLK):
    t, d = x.shape
    ng = w_gates.shape[1]
    return pl.pallas_call(
        _norm1_kernel,
        grid=(t // tm,),
        in_specs=[
            pl.BlockSpec((tm, d), lambda i: (i, 0)),
            pl.BlockSpec((1, d), lambda i: (0, 0)),
            _mod_spec(grp, layer, d, 0, tm), _mod_spec(grp, layer, d, 1, tm),
            pl.BlockSpec((d, ng), lambda i: (0, 0)),
        ],
        out_specs=[pl.BlockSpec((tm, d), lambda i: (i, 0)), pl.BlockSpec((tm, ng), lambda i: (i, 0))],
        out_shape=[jax.ShapeDtypeStruct((t, d), BF16), jax.ShapeDtypeStruct((t, ng), F32)],
        compiler_params=_cparams("arbitrary"),
        name="norm1_gates",
    )(x, g.reshape(1, d), mod, mod, w_gates)


def _mm_kernel(x_ref, w_ref, o_ref):
    o_ref[...] = _dot(x_ref[...], w_ref[...])


def matmul(x, w, tm=1024, tn=1024, name="matmul"):
    m, k = x.shape
    n = w.shape[1]
    tm, tn = _pick(m, tm), _pick(n, tn)
    return pl.pallas_call(
        _mm_kernel,
        grid=(m // tm, n // tn),
        in_specs=[pl.BlockSpec((tm, k), lambda i, j: (i, 0)), pl.BlockSpec((k, tn), lambda i, j: (0, j))],
        out_specs=pl.BlockSpec((tm, tn), lambda i, j: (i, j)),
        out_shape=jax.ShapeDtypeStruct((m, n), F32),
        compiler_params=_cparams("arbitrary", "arbitrary"),
        name=name,
    )(x, w)


def _dwconv(x, w):
    l = x.shape[0]
    row = lax.broadcasted_iota(I32, x.shape, 0)
    acc = x * w[2:3, :]
    for j, s in ((0, -2), (1, -1), (3, 1)):
        shifted = pltpu.roll(x, shift=(-s) % l, axis=0)
        valid = (row + s >= 0) & (row + s < l)
        acc = acc + jnp.where(valid, shifted, 0.0) * w[j:j + 1, :]
    return acc


_G_BETA, _G_EG, _G_EKD, _G_EGL, _G_GC = 0, 1, 2, 3, 4


def _unit_tri_inverse(a, lower):
    n = a.shape[0]
    ri = lax.broadcasted_iota(I32, (n, n), 0)
    ci = lax.broadcasted_iota(I32, (n, n), 1)
    t = None
    s = 1
    while s < DN_CHUNK:
        first, second = (ci, ri) if lower else (ri, ci)
        pair = ((ri // (2 * s)) == (ci // (2 * s))) & ((first // s) % 2 == 0) & ((second // s) % 2 == 1)
        a_off = jnp.where(pair, a, 0.0)
        if t is None:
            t = (ri == ci).astype(F32) - a_off
        else:
            t = t - _dot3(_dot3(t, a_off), t)
        s *= 2
    return t


def _dn_kernel(q_ref, k_ref, v_ref, z_ref, gc_ref, gt_ref, cq_ref, ck_ref, cv_ref, ng_ref, s0_ref,
               o_ref, sfin_ref, qs, ks, vs, os_f, os_b, *, nb, dk, heads):
    h = pl.program_id(1)
    blk = ROW_BLK
    c_per = blk // DN_CHUNK

    qs[...] = _l2n(_silu(_dwconv(q_ref[...], cq_ref[...]))) * (dk ** -0.5)
    ks[...] = _l2n(_silu(_dwconv(k_ref[...], ck_ref[...])))
    vs[...] = _silu(_dwconv(v_ref[...], cv_ref[...]))

    ri = lax.broadcasted_iota(I32, (blk, blk), 0)
    ci = lax.broadcasted_iota(I32, (blk, blk), 1)
    same_chunk = (ri // DN_CHUNK) == (ci // DN_CHUNK)

    def block_dir(j, d, s):
        r0 = j * blk
        if not isinstance(r0, int):
            r0 = pl.multiple_of(r0, blk)
        rows = pl.ds(r0, blk)
        q, k, v = qs[rows, :], ks[rows, :], vs[rows, :]
        g = gc_ref[rows, :]
        base = 5 * d
        beta = g[:, base + _G_BETA:base + _G_BETA + 1]
        eg = g[:, base + _G_EG:base + _G_EG + 1]
        ekd = g[:, base + _G_EKD:base + _G_EKD + 1]
        egl = g[:, base + _G_EGL:base + _G_EGL + 1]
        gcc = g[:, base + _G_GC:base + _G_GC + 1]
        gcr = gt_ref[j, pl.ds(d * heads + h, 1), :]
        incl = same_chunk & ((ri >= ci) if d == 0 else (ri <= ci))
        strict = same_chunk & ((ri > ci) if d == 0 else (ri < ci))
        decay = jnp.where(incl, jnp.exp(jnp.where(incl, gcc - gcr, 0.0)), 0.0)
        kbf = k.astype(BF16)
        kb = k * beta
        a = jnp.where(strict, _dot_nt(kb.astype(BF16), kbf) * decay, 0.0)
        t = _unit_tri_inverse(a, lower=(d == 0)).astype(BF16)
        u = _dot(t, (v * beta).astype(BF16))
        w = _dot(t, (kb * eg).astype(BF16))
        qk = _dot_nt(q.astype(BF16), kbf) * decay
        q_dec = q * eg
        k_dec = k * ekd
        order = range(c_per) if d == 0 else range(c_per - 1, -1, -1)
        outs = [None] * c_per
        for c in order:
            cs = slice(c * DN_CHUNK, (c + 1) * DN_CHUNK)
            sb = s.astype(BF16)
            v_new = u[cs, :] - _dot(w[cs, :].astype(BF16), sb)
            vnb = v_new.astype(BF16)
            outs[c] = _dot(q_dec[cs, :].astype(BF16), sb) + _dot(qk[cs, cs].astype(BF16), vnb)
            s = s * egl[c * DN_CHUNK:c * DN_CHUNK + 1, :] + _dot_tn(k_dec[cs, :].astype(BF16), vnb)
        dst = os_f if d == 0 else os_b
        dst[rows, :] = jnp.concatenate(outs, axis=0)
        return s

    s_f = s0_ref[0]
    s_b = s0_ref[1]
    if nb == 1:
        s_f = block_dir(0, 0, s_f)
        s_b = block_dir(0, 1, s_b)
    else:
        def body(j, carry):
            return block_dir(j, 0, carry[0]), block_dir(nb - 1 - j, 1, carry[1])
        s_f, s_b = lax.fori_loop(0, nb, body, (s_f, s_b))
    sfin_ref[0] = s_f
    sfin_ref[1] = s_b

    o = os_f[...] + os_b[...]
    o = o * lax.rsqrt(jnp.mean(o * o, axis=-1, keepdims=True) + EPS) * ng_ref[...]
    o_ref[...] = (o * _silu(z_ref[...])).astype(BF16)


def deltanet(proj, gcol, gt, conv_w, norm_g, s0, *, n_seq, l, row0, heads, dk, cols):
    nb = l // ROW_BLK
    rb0 = row0 // l
    cq, ck, cv, cz = (c // LANES for c in cols)

    def col(cb):
        return pl.BlockSpec((l, LANES), lambda b, h: (rb0 + b, cb + h))

    def cw(off):
        return pl.BlockSpec((4, LANES), lambda b, h: (0, off + h))

    state_spec = pl.BlockSpec((None, 2, None, dk, LANES), lambda b, h: (b, 0, h, 0, 0))
    return pl.pallas_call(
        functools.partial(_dn_kernel, nb=nb, dk=dk, heads=heads),
        grid=(n_seq, heads),
        in_specs=[
            col(cq), col(ck), col(cv), col(cz),
            pl.BlockSpec((l, LANES), lambda b, h: (rb0 + b, h)),
            pl.BlockSpec((nb, gt.shape[1], ROW_BLK), lambda b, h: (rb0 + b, 0, 0)),
            cw(0), cw(heads), cw(2 * heads),
            pl.BlockSpec((1, LANES), lambda b, h: (0, 0)),
            state_spec,
        ],
        out_specs=[pl.BlockSpec((l, LANES), lambda b, h: (b, h)), state_spec],
        out_shape=[jax.ShapeDtypeStruct((n_seq * l, heads * LANES), BF16),
                   jax.ShapeDtypeStruct((n_seq, 2, heads, dk, LANES), F32)],
        scratch_shapes=[pltpu.VMEM((l, LANES), F32)] * 5,
        compiler_params=_cparams("arbitrary", "arbitrary"),
        name=f"deltanet_l{l}",
    )(proj, proj, proj, proj, gcol, gt, conv_w, conv_w, conv_w, norm_g.reshape(1, LANES), s0)


def dn_gate_prep(gates, a_log, dt_bias, heads):
    t = gates.shape[0]
    n = t // DN_CHUNK
    beta = jax.nn.sigmoid(gates[:, :2 * heads]).reshape(n, DN_CHUNK, 2, heads)
    g = -jnp.exp(a_log)[None] * jax.nn.softplus(gates[:, 2 * heads:4 * heads].reshape(t, 2, heads) + dt_bias[None])
    g = g.reshape(n, DN_CHUNK, 2, heads)
    gc_f = jnp.cumsum(g[:, :, 0], axis=1)
    gc_b = jnp.flip(jnp.cumsum(jnp.flip(g[:, :, 1], axis=1), axis=1), axis=1)
    gc = jnp.stack([gc_f, gc_b], axis=2)
    glast = jnp.stack([gc_f[:, -1], gc_b[:, 0]], axis=1)[:, None]
    quant = jnp.stack([beta, jnp.exp(gc), jnp.exp(glast - gc),
                       jnp.broadcast_to(jnp.exp(glast), gc.shape), gc], axis=-1)
    quant = jnp.transpose(quant, (0, 1, 3, 2, 4)).reshape(t, heads, 10)
    gcol = jnp.pad(quant, ((0, 0), (0, 0), (0, LANES - 10))).reshape(t, heads * LANES)
    gt = jnp.transpose(gc.reshape(t // ROW_BLK, ROW_BLK, 2 * heads), (0, 2, 1))
    return gcol, gt


def _lru_kernel(x_ref, y_ref, cw_ref, cb_ref, wa_ref, wx_ref, ba_ref, bx_ref, lam_ref, h0_ref,
                o_ref, sfin_ref, a_s, b_s, h_s, *, l, cw):
    xc = _dwconv(x_ref[...], cw_ref[...]) + cb_ref[...]
    for d in range(2):
        sp = _softplus(-lam_ref[d:d + 1, :])
        for s in range(cw // LANES):
            cs = slice(s * LANES, (s + 1) * LANES)
            xs = xc[:, cs]
            xb = xs.astype(BF16)
            r = _sigmoid(_dot(xb, wa_ref[d, s].astype(BF16)) + ba_ref[d:d + 1, cs])
            i = _sigmoid(_dot(xb, wx_ref[d, s].astype(BF16)) + bx_ref[d:d + 1, cs])
            log_a = -LRU_C * r * sp[:, cs]
            a = jnp.exp(log_a)
            a_s[d, :, cs] = a
            b_s[d, :, cs] = jnp.sqrt(jnp.tanh(-log_a) * (a * a + 1.0)) * i * xs

    def fwd(t, h):
        h = a_s[0, pl.ds(t, 1), :] * h + b_s[0, pl.ds(t, 1), :]
        h_s[pl.ds(t, 1), :] = h
        return h

    def bwd(i, h):
        t = l - 1 - i
        h = a_s[1, pl.ds(t, 1), :] * h + b_s[1, pl.ds(t, 1), :]
        h_s[pl.ds(t, 1), :] = h_s[pl.ds(t, 1), :] + h
        return h

    sfin_ref[0:1, :] = lax.fori_loop(0, l, fwd, h0_ref[0:1, :], unroll=8)
    sfin_ref[1:2, :] = lax.fori_loop(0, l, bwd, h0_ref[1:2, :], unroll=8)
    o_ref[...] = (h_s[...] * _gelu_tanh(y_ref[...])).astype(BF16)


def rglru(proj, conv_w, conv_b, wa, wx, ba, bx, lam, h0, *, n_seq, l, row0, cols, cw=512):
    width = conv_w.shape[1]
    cw = _pick(width, cw)
    rb0 = row0 // l
    cx, cy = (c // cw for c in cols)
    nsub = cw // LANES
    vec2 = pl.BlockSpec((2, cw), lambda b, n: (0, n))
    wblk = pl.BlockSpec((2, nsub, LANES, LANES), lambda b, n: (0, n, 0, 0))
    state_spec = pl.BlockSpec((None, 2, cw), lambda b, n: (b, 0, n))
    return pl.pallas_call(
        functools.partial(_lru_kernel, l=l, cw=cw),
        grid=(n_seq, width // cw),
        in_specs=[
            pl.BlockSpec((l, cw), lambda b, n: (rb0 + b, cx + n)),
            pl.BlockSpec((l, cw), lambda b, n: (rb0 + b, cy + n)),
            pl.BlockSpec((4, cw), lambda b, n: (0, n)),
            pl.BlockSpec((1, cw), lambda b, n: (0, n)),
            wblk, wblk, vec2, vec2, vec2, state_spec,
        ],
        out_specs=[pl.BlockSpec((l, cw), lambda b, n: (b, n)), state_spec],
        out_shape=[jax.ShapeDtypeStruct((n_seq * l, width), BF16),
                   jax.ShapeDtypeStruct((n_seq, 2, width), F32)],
        scratch_shapes=[pltpu.VMEM((2, l, cw), F32), pltpu.VMEM((2, l, cw), F32), pltpu.VMEM((l, cw), F32)],
        compiler_params=_cparams("arbitrary", "arbitrary"),
        name=f"rglru_l{l}",
    )(proj, proj, conv_w, conv_b.reshape(1, width), wa, wx, ba, bx, lam, h0)


def _branch_kernel(a_ref, b_ref, wa_ref, wb_ref, ga_ref, gb_ref, o_ref):
    oa = _dot(a_ref[...], wa_ref[...])
    ob = _dot(b_ref[...], wb_ref[...])
    o_ref[...] = (_sigmoid(ga_ref[...]) * oa + _sigmoid(gb_ref[...]) * ob).astype(BF16)


def branch_merge(dn_out, lru_out, wa, wb, proj, cols, tm=1024, tn=512):
    t, ka = dn_out.shape
    kb = lru_out.shape[1]
    d = wa.shape[1]
    tm, tn = _pick(t, tm), _pick(d, tn)
    cga, cgb = (c // tn for c in cols)
    return pl.pallas_call(
        _branch_kernel,
        grid=(t // tm, d // tn),
        in_specs=[
            pl.BlockSpec((tm, ka), lambda i, j: (i, 0)),
            pl.BlockSpec((tm, kb), lambda i, j: (i, 0)),
            pl.BlockSpec((ka, tn), lambda i, j: (0, j)),
            pl.BlockSpec((kb, tn), lambda i, j: (0, j)),
            pl.BlockSpec((tm, tn), lambda i, j: (i, cga + j)),
            pl.BlockSpec((tm, tn), lambda i, j: (i, cgb + j)),
        ],
        out_specs=pl.BlockSpec((tm, tn), lambda i, j: (i, j)),
        out_shape=jax.ShapeDtypeStruct((t, d), BF16),
        compiler_params=_cparams("arbitrary", "arbitrary"),
        name="branch_merge",
    )(dn_out, lru_out, wa, wb, proj, proj)


def _resid_kernel(m_ref, w_ref, x_ref, g_ref, o_ref):
    o_ref[...] = x_ref[...] + g_ref[...] * _dot(m_ref[...], w_ref[...])


def out_proj_residual(m, w, x, mod, layer, grp, tm=1024, tn=512):
    t, k = m.shape
    d = w.shape[1]
    tm, tn = grp.row_tile(tm), _pick(d, tn)
    return pl.pallas_call(
        _resid_kernel,
        grid=(t // tm, d // tn),
        in_specs=[
            pl.BlockSpec((tm, k), lambda i, j: (i, 0)),
            pl.BlockSpec((k, tn), lambda i, j: (0, j)),
            pl.BlockSpec((tm, tn), lambda i, j: (i, j)),
            _mod_spec(grp, layer, d, 2, tm, ncols=tn, two_d=True),
        ],
        out_specs=pl.BlockSpec((tm, tn), lambda i, j: (i, j)),
        out_shape=jax.ShapeDtypeStruct((t, d), F32),
        compiler_params=_cparams("arbitrary", "arbitrary"),
        name="out_proj_residual",
    )(m, w, x, mod)


def _first_argmax(x, idx, n, axis):
    m = jnp.max(x, axis=axis, keepdims=True)
    return jnp.min(jnp.where(x == m, idx, n), axis=axis, keepdims=True)


def _norm2_kernel(x_ref, g_ref, sh_ref, sc_ref, rw_ref, rb_ref, h_ref, hb_ref, idx_ref, w_ref):
    h = _rms_mod(x_ref[...], g_ref[...], sc_ref[...], sh_ref[...])
    h_ref[...] = h
    hb_ref[...] = h.astype(BF16)
    tm = h.shape[0]
    n_exp = rw_ref.shape[0]
    per = n_exp // N_GROUPS
    scores = _sigmoid(_dot3(rw_ref[...], h, dot=_dot_nt))
    sel = scores + rb_ref[...]
    neg = -jnp.inf
    grp = sel.reshape(N_GROUPS, per, tm)
    sub = lax.broadcasted_iota(I32, grp.shape, 1)
    m1 = jnp.max(grp, axis=1, keepdims=True)
    first = jnp.min(jnp.where(grp == m1, sub, per), axis=1, keepdims=True)
    m2 = jnp.max(jnp.where(sub == first, neg, grp), axis=1, keepdims=True)
    gscore = (m1 + m2).reshape(N_GROUPS, tm)
    gi = lax.broadcasted_iota(I32, gscore.shape, 0)
    gmask = jnp.zeros(gscore.shape, F32)
    for _ in range(TOPK_GROUPS):
        pick = gi == _first_argmax(gscore, gi, N_GROUPS, 0)
        gmask = jnp.where(pick, 1.0, gmask)
        gscore = jnp.where(pick, neg, gscore)
    emask = jnp.broadcast_to(gmask.reshape(N_GROUPS, 1, tm), grp.shape).reshape(n_exp, tm)
    cand = jnp.where(emask > 0.0, sel, neg)
    ei = lax.broadcasted_iota(I32, cand.shape, 0)
    idx_rows, w_rows = [], []
    for _ in range(TOP_K):
        f = _first_argmax(cand, ei, n_exp, 0)
        pick = ei == f
        idx_rows.append(f)
        w_rows.append(jnp.sum(jnp.where(pick, scores, 0.0), axis=0, keepdims=True))
        cand = jnp.where(pick, neg, cand)
    tw = jnp.concatenate(w_rows, axis=0)
    tw = tw / jnp.sum(tw, axis=0, keepdims=True) * ROUTED_SCALE
    pad = 8 - TOP_K
    idx_ref[...] = jnp.concatenate(idx_rows + [jnp.zeros((pad, tm), I32)], axis=0)
    w_ref[...] = jnp.concatenate([tw, jnp.zeros((pad, tm), F32)], axis=0)


def norm2_router(x, g, mod, layer, router_wt, router_bias, grp, tm=ROW_BLK):
    t, d = x.shape
    n_exp = router_wt.shape[0]
    return pl.pallas_call(
        _norm2_kernel,
        grid=(t // tm,),
        in_specs=[
            pl.BlockSpec((tm, d), lambda i: (i, 0)),
            pl.BlockSpec((1, d), lambda i: (0, 0)),
            _mod_spec(grp, layer, d, 3, tm), _mod_spec(grp, layer, d, 4, tm),
            pl.BlockSpec((n_exp, d), lambda i: (0, 0)),
            pl.BlockSpec((n_exp, 1), lambda i: (0, 0)),
        ],
        out_specs=[pl.BlockSpec((tm, d), lambda i: (i, 0)), pl.BlockSpec((tm, d), lambda i: (i, 0)),
                   pl.BlockSpec((8, tm), lambda i: (0, i)), pl.BlockSpec((8, tm), lambda i: (0, i))],
        out_shape=[jax.ShapeDtypeStruct((t, d), F32), jax.ShapeDtypeStruct((t, d), BF16),
                   jax.ShapeDtypeStruct((8, t), I32), jax.ShapeDtypeStruct((8, t), F32)],
        compiler_params=_cparams("arbitrary"),
        name="norm2_router",
    )(x, g.reshape(1, d), mod, mod, router_wt, router_bias.reshape(n_exp, 1))


def route_metadata(top_idx, top_w, n_exp, tm):
    k, t = top_idx.shape
    n_pairs = k * t
    n_tiles = n_pairs // tm + n_exp
    n_rows = n_tiles * tm
    e_flat = top_idx.reshape(-1)
    onehot = (e_flat[:, None] == jnp.arange(n_exp, dtype=I32)[None, :]).astype(I32)
    csum = jnp.cumsum(onehot, axis=0)
    counts = csum[-1]
    rank = jnp.take_along_axis(csum, e_flat[:, None], axis=1)[:, 0] - 1
    padded = (counts + tm - 1) // tm * tm
    ends = jnp.cumsum(padded)
    off = ends - padded
    pos = off[e_flat] + rank
    pair = jnp.arange(n_pairs, dtype=I32)
    row_token = jnp.zeros((n_rows,), I32).at[pos].set(pair % t)
    rows = jnp.arange(n_rows, dtype=I32)
    spare = n_pairs + (rows // tm) % 2 * 8 + rows % 8
    row_dst = spare.at[pos].set(pair)
    row_gate = jnp.zeros((n_rows,), F32).at[pos].set(top_w.reshape(-1))
    n_valid = ends[-1] // tm
    start = jnp.arange(n_tiles, dtype=I32) * tm
    te = jnp.minimum(jnp.searchsorted(ends, start, side="right").astype(I32), n_exp - 1)
    last_e = te[jnp.maximum(n_valid - 1, 0)]
    valid = jnp.arange(n_tiles, dtype=I32) < n_valid
    te = jnp.where(valid, te, last_e)
    rows_valid = jnp.where(valid, jnp.clip(counts[te] - (start - off[te]), 0, tm), 0).astype(I32)
    rows_valid = (rows_valid + 7) // 8 * 8
    return dict(tile_expert=te, n_valid=n_valid.reshape(1).astype(I32), rows_valid=rows_valid,
                row_token=row_token.reshape(n_tiles, 1, tm), row_dst=row_dst.reshape(n_tiles, 1, tm),
                row_gate=row_gate.reshape(n_rows, 1), n_tiles=n_tiles)


def _moe1_kernel(te_ref, nv_ref, tok_ref, tok_next_ref, h_hbm, wg_ref, wu_ref, gate_ref, o_ref,
                 xbuf, sem, *, tm):
    del te_ref
    i = pl.program_id(0)
    n_valid = nv_ref[0]
    slot = i % 2

    def row_copy(tok, s, r):
        return pltpu.make_async_copy(h_hbm.at[pl.ds(tok, 1), :], xbuf.at[s, pl.ds(r, 1), :], sem.at[s])

    def gather(tref, s):
        def body(r, carry):
            row_copy(tref[0, r], s, r).start()
            return carry
        lax.fori_loop(0, tm, body, 0, unroll=8)

    @pl.when((i == 0) & (n_valid > 0))
    def _():
        gather(tok_ref, 0)

    @pl.when(i + 1 < n_valid)
    def _():
        gather(tok_next_ref, 1 - slot)

    @pl.when(i < n_valid)
    def _():
        pltpu.make_async_copy(h_hbm.at[pl.ds(0, tm), :], xbuf.at[slot], sem.at[slot]).wait()
        x = xbuf[slot].astype(BF16)
        g = _dot(x, wg_ref[...].astype(BF16))
        u = _dot(x, wu_ref[...].astype(BF16))
        o_ref[...] = (_silu(g) * u * gate_ref[...]).astype(BF16)

    @pl.when(i >= n_valid)
    def _():
        o_ref[...] = jnp.zeros(o_ref.shape, BF16)


def moe_stage1(h2, w_gate, w_up, layer, meta, tm=MOE_TM):
    t, d = h2.shape
    de = w_gate.shape[-1]
    n_tiles = meta["n_tiles"]
    last = n_tiles - 1
    wspec = pl.BlockSpec((None, None, d, de), lambda i, te, nv: (layer, te[i], 0, 0))
    grid_spec = pltpu.PrefetchScalarGridSpec(
        num_scalar_prefetch=2,
        grid=(n_tiles,),
        in_specs=[
            pl.BlockSpec((None, 1, tm), lambda i, te, nv: (i, 0, 0), memory_space=pltpu.SMEM),
            pl.BlockSpec((None, 1, tm), lambda i, te, nv: (jnp.minimum(i + 1, last), 0, 0),
                         memory_space=pltpu.SMEM),
            pl.BlockSpec(memory_space=pl.ANY),
            wspec, wspec,
            pl.BlockSpec((tm, 1), lambda i, te, nv: (i, 0)),
        ],
        out_specs=pl.BlockSpec((tm, de), lambda i, te, nv: (i, 0)),
        scratch_shapes=[pltpu.VMEM((2, tm, d), F32), pltpu.SemaphoreType.DMA((2,))],
    )
    return pl.pallas_call(
        functools.partial(_moe1_kernel, tm=tm),
        grid_spec=grid_spec,
        out_shape=jax.ShapeDtypeStruct((n_tiles * tm, de), BF16),
        compiler_params=_cparams("arbitrary"),
        name="moe_stage1",
    )(meta["tile_expert"], meta["n_valid"], meta["row_token"], meta["row_token"], h2, w_gate, w_up,
      meta["row_gate"])


def _moe2_kernel(te_ref, nv_ref, nr_ref, dst_ref, hm_ref, wd_ref, out_hbm, ybuf, sem, *, tm, n_tiles):
    del te_ref
    i = pl.program_id(0)
    n_valid = nv_ref[0]
    slot = i % 2

    def wait_tile(j, s):
        n = pl.multiple_of(nr_ref[j], 8)
        pltpu.make_async_copy(ybuf.at[s, pl.ds(0, n), :], out_hbm.at[pl.ds(0, n), :], sem.at[s]).wait()

    @pl.when(i == 0)
    def _():
        spare = out_hbm.shape[0] - 16
        ybuf[1, 0:16, :] = jnp.zeros((16, ybuf.shape[2]), F32)
        init = pltpu.make_async_copy(ybuf.at[1, pl.ds(0, 16), :], out_hbm.at[pl.ds(spare, 16), :], sem.at[1])
        init.start()
        init.wait()

    @pl.when((i >= 2) & (i - 2 < n_valid))
    def _():
        wait_tile(i - 2, slot)

    @pl.when(i < n_valid)
    def _():
        ybuf[slot] = _dot(hm_ref[...], wd_ref[...].astype(BF16))

        def body(r, carry):
            pltpu.make_async_copy(ybuf.at[slot, pl.ds(r, 1), :], out_hbm.at[pl.ds(dst_ref[0, r], 1), :],
                                  sem.at[slot]).start()
            return carry
        lax.fori_loop(0, nr_ref[i], body, 0)

    @pl.when(i == n_tiles - 1)
    def _():
        @pl.when((i >= 1) & (i - 1 < n_valid))
        def _():
            wait_tile(i - 1, 1 - slot)

        @pl.when(i < n_valid)
        def _():
            wait_tile(i, slot)


def moe_stage2(hmid, w_down, layer, meta, n_out_rows, tm=MOE_TM):
    de = hmid.shape[1]
    d = w_down.shape[-1]
    n_tiles = meta["n_tiles"]
    grid_spec = pltpu.PrefetchScalarGridSpec(
        num_scalar_prefetch=3,
        grid=(n_tiles,),
        in_specs=[
            pl.BlockSpec((None, 1, tm), lambda i, te, nv, nr: (i, 0, 0), memory_space=pltpu.SMEM),
            pl.BlockSpec((tm, de), lambda i, te, nv, nr: (i, 0)),
            pl.BlockSpec((None, None, de, d), lambda i, te, nv, nr: (layer, te[i], 0, 0)),
        ],
        out_specs=pl.BlockSpec(memory_space=pl.ANY),
        scratch_shapes=[pltpu.VMEM((2, tm, d), F32), pltpu.SemaphoreType.DMA((2,))],
    )
    return pl.pallas_call(
        functools.partial(_moe2_kernel, tm=tm, n_tiles=n_tiles),
        grid_spec=grid_spec,
        out_shape=jax.ShapeDtypeStruct((n_out_rows, d), F32),
        compiler_params=_cparams("arbitrary"),
        name="moe_stage2",
    )(meta["tile_expert"], meta["n_valid"], meta["rows_valid"], meta["row_dst"], hmid, w_down)


def _swiglu_in_kernel(x_ref, wg_ref, wu_ref, o_ref):
    x = x_ref[...]
    o_ref[...] = (_silu(_dot(x, wg_ref[...])) * _dot(x, wu_ref[...])).astype(BF16)


def shared_in(hb, wg, wu, tm=1024, tn=512):
    t, d = hb.shape
    n = wg.shape[1]
    tm, tn = _pick(t, tm), _pick(n, tn)
    wspec = pl.BlockSpec((d, tn), lambda i, j: (0, j))
    return pl.pallas_call(
        _swiglu_in_kernel,
        grid=(t // tm, n // tn),
        in_specs=[pl.BlockSpec((tm, d), lambda i, j: (i, 0)), wspec, wspec],
        out_specs=pl.BlockSpec((tm, tn), lambda i, j: (i, j)),
        out_shape=jax.ShapeDtypeStruct((t, n), BF16),
        compiler_params=_cparams("arbitrary", "arbitrary"),
        name="shared_in",
    )(hb, wg, wu)


def _moe_out_kernel(hs_ref, w_ref, x_ref, g_ref, *rest):
    slot_refs, o_ref = rest[:-1], rest[-1]
    routed = slot_refs[0][...]
    for r in slot_refs[1:]:
        routed = routed + r[...]
    o_ref[...] = x_ref[...] + g_ref[...] * (routed + _dot(hs_ref[...], w_ref[...]))


def moe_out_residual(hs, w, x, mod, layer, slots, grp, tm=512, tn=1024):
    t, k = hs.shape
    d = w.shape[1]
    tm, tn = grp.row_tile(tm), _pick(d, tn)
    n_slots = TOP_K

    def slot_spec(s):
        return pl.BlockSpec((tm, tn), lambda i, j: (s * (t // tm) + i, j))

    return pl.pallas_call(
        _moe_out_kernel,
        grid=(t // tm, d // tn),
        in_specs=[
            pl.BlockSpec((tm, k), lambda i, j: (i, 0)),
            pl.BlockSpec((k, tn), lambda i, j: (0, j)),
            pl.BlockSpec((tm, tn), lambda i, j: (i, j)),
            _mod_spec(grp, layer, d, 5, tm, ncols=tn, two_d=True),
        ] + [slot_spec(s) for s in range(n_slots)],
        out_specs=pl.BlockSpec((tm, tn), lambda i, j: (i, j)),
        out_shape=jax.ShapeDtypeStruct((t, d), F32),
        compiler_params=_cparams("arbitrary", "arbitrary"),
        name="moe_out_residual",
    )(hs, w, x, mod, *([slots] * n_slots))


def _final_norm_kernel(x_ref, g_ref, o_ref):
    x = x_ref[...]
    o_ref[...] = x * lax.rsqrt(jnp.mean(x * x, axis=-1, keepdims=True) + EPS) * g_ref[...]


def final_norm(x, g, tm=512):
    t, d = x.shape
    tm = _pick(t, tm)
    return pl.pallas_call(
        _final_norm_kernel,
        grid=(t // tm,),
        in_specs=[pl.BlockSpec((tm, d), lambda i: (i, 0)), pl.BlockSpec((1, d), lambda i: (0, 0))],
        out_specs=pl.BlockSpec((tm, d), lambda i: (i, 0)),
        out_shape=jax.ShapeDtypeStruct((t, d), F32),
        compiler_params=_cparams("arbitrary"),
        name="final_norm",
    )(x, g.reshape(1, d))


def _grid_sincos(n_tokens, dim):
    rows = n_tokens // GRID_W
    quarter = dim // 4
    omega = 1.0 / (10000.0 ** (jnp.arange(quarter, dtype=F32) / quarter))
    r = jnp.broadcast_to(jnp.arange(rows, dtype=F32)[:, None, None] * omega, (rows, GRID_W, quarter))
    cl = jnp.broadcast_to(jnp.arange(GRID_W, dtype=F32)[None, :, None] * omega, (rows, GRID_W, quarter))
    emb = jnp.concatenate([jnp.sin(r), jnp.cos(r), jnp.sin(cl), jnp.cos(cl)], axis=-1)
    return emb.reshape(n_tokens, dim)


def kernel(x_prompt, x_sample, state_delta, state_lru, c, c_ctx, w_ada, b_ada, norm1_g, w_in, dn_conv_w,
           dn_a_log, dn_dt_bias, dn_norm_g, w_branch_a, lru_conv_w, lru_conv_b, lru_wa, lru_ba, lru_wx,
           lru_bx, lru_lambda, w_branch_b, w_out, norm2_g, router_w, router_bias, w_exp_gate, w_exp_up,
           w_exp_down, w_sh_gate, w_sh_up, w_sh_down, final_norm_g):
    n_p, l_p, d = x_prompt.shape
    n_s, l_s, _ = x_sample.shape
    depth = w_ada.shape[0]
    heads, dk = state_delta.shape[3], state_delta.shape[4]
    qk_w = heads * dk
    v_w = heads * state_delta.shape[5]
    lru_w = state_lru.shape[-1]
    n_exp = router_w.shape[-1]
    grp = Groups(n_p, l_p, n_s, l_s)
    t = grp.t
    assert l_p % ROW_BLK == 0 and l_s % ROW_BLK == 0 and dk == LANES and v_w == qk_w

    x = jnp.concatenate([x_prompt.reshape(n_p * l_p, d),
                         (x_sample + _grid_sincos(l_s, d)[None]).reshape(n_s * l_s, d)], axis=0)
    cond = jnp.concatenate([c_ctx[None], c, jnp.zeros((MOD_ROWS - 1 - n_s, d), F32)], axis=0)
    mod = ada_mod(cond, w_ada, b_ada).reshape(depth * MOD_ROWS, 1, 6 * d)

    g0 = 2 * qk_w + 2 * v_w
    g1 = g0 + 4 * heads
    col_q, col_k, col_v, col_z = 0, qk_w, 2 * qk_w, 2 * qk_w + v_w
    col_lx, col_ly = g0, g0 + lru_w
    col_ga, col_gb = g0 + 2 * lru_w, g0 + 2 * lru_w + d

    zero_delta = jnp.zeros((n_p, 2, heads, dk, LANES), F32)
    zero_lru = jnp.zeros((n_p, 2, lru_w), F32)
    new_delta, new_lru = [], []
    for l in range(depth):
        w_main = jnp.concatenate([w_in[l][:, :g0], w_in[l][:, g1:]], axis=1).astype(BF16)
        w_gates = jnp.pad(w_in[l][:, g0:g1], ((0, 0), (0, LANES - 4 * heads)))

        h, gates = norm1_gates(x, norm1_g[l], mod, l, w_gates, grp)
        proj = matmul(h, w_main, name="in_proj")
        gcol, gt = dn_gate_prep(gates, dn_a_log[l], dn_dt_bias[l], heads)

        dn_kw = dict(heads=heads, dk=dk, cols=(col_q, col_k, col_v, col_z))
        dn_p, s_delta = deltanet(proj, gcol, gt, dn_conv_w[l], dn_norm_g[l], zero_delta,
                                 n_seq=n_p, l=l_p, row0=0, **dn_kw)
        dn_s, _ = deltanet(proj, gcol, gt, dn_conv_w[l], dn_norm_g[l], state_delta[:, l],
                           n_seq=n_s, l=l_s, row0=grp.t_p, **dn_kw)
        lru_args = (proj, lru_conv_w[l], lru_conv_b[l], lru_wa[l], lru_wx[l], lru_ba[l], lru_bx[l],
                    lru_lambda[l])
        lru_p, s_lru = rglru(*lru_args, zero_lru, n_seq=n_p, l=l_p, row0=0, cols=(col_lx, col_ly))
        lru_s, _ = rglru(*lru_args, state_lru[:, l], n_seq=n_s, l=l_s, row0=grp.t_p, cols=(col_lx, col_ly))
        new_delta.append(s_delta)
        new_lru.append(s_lru)

        merged = branch_merge(jnp.concatenate([dn_p, dn_s], axis=0), jnp.concatenate([lru_p, lru_s], axis=0),
                              w_branch_a[l].astype(BF16), w_branch_b[l].astype(BF16), proj, (col_ga, col_gb))
        x = out_proj_residual(merged, w_out[l].astype(BF16), x, mod, l, grp)

        h2, h2b, top_idx, top_w = norm2_router(x, norm2_g[l], mod, l, router_w[l].T, router_bias[l], grp)
        meta = route_metadata(top_idx[:TOP_K], top_w[:TOP_K], n_exp, MOE_TM)
        hmid = moe_stage1(h2, w_exp_gate, w_exp_up, l, meta)
        slots = moe_stage2(hmid, w_exp_down, l, meta, TOP_K * t + 16)
        hs = shared_in(h2b, w_sh_gate[l].astype(BF16), w_sh_up[l].astype(BF16))
        x = moe_out_residual(hs, w_sh_down[l].astype(BF16), x, mod, l, slots, grp)

    y = final_norm(x, final_norm_g)
    y_prompt = y[:grp.t_p].reshape(n_p, l_p, d)
    y_sample = y[grp.t_p:].reshape(n_s, l_s, d)
    return (y_prompt, y_sample, jnp.stack(new_delta, axis=1), jnp.stack(new_lru, axis=1))
```

```python
import functools
import math

import jax
import jax.numpy as jnp
from jax import lax
from jax.experimental import pallas as pl
from jax.experimental.pallas import tpu as pltpu

F32 = jnp.float32
BF16 = jnp.bfloat16
I32 = jnp.int32

EPS = 1e-6
GRID_W = 64
DN_CHUNK = 64
LRU_C = 8.0
TOP_K = 6
N_GROUPS = 8
TOPK_GROUPS = 4
ROUTED_SCALE = 2.5

V7X_VMEM_LIMIT_BYTES = 56 * 1024 * 1024
LANES = 128
ROW_BLK = 256
MOE_TM = 256
MOD_ROWS = 8


def _cparams(*sem):
    return pltpu.CompilerParams(dimension_semantics=sem, vmem_limit_bytes=V7X_VMEM_LIMIT_BYTES)


def _dot(a, b):
    return jnp.dot(a, b, preferred_element_type=F32)


def _dot_nt(a, b):
    return lax.dot_general(a, b, (((1,), (1,)), ((), ())), preferred_element_type=F32)


def _dot_tn(a, b):
    return lax.dot_general(a, b, (((0,), (0,)), ((), ())), preferred_element_type=F32)


def _split(a):
    hi = a.astype(BF16)
    lo = (a - hi.astype(F32)).astype(BF16)
    return hi, lo


def _dot3(a, b, dot=_dot):
    ah, al = _split(a)
    bh, bl = _split(b)
    return dot(ah, bh) + (dot(ah, bl) + dot(al, bh))


def _sigmoid(x):
    return 1.0 / (1.0 + jnp.exp(-x))


def _silu(x):
    return x * _sigmoid(x)


def _softplus(x):
    return jnp.maximum(x, 0.0) + jnp.log1p(jnp.exp(-jnp.abs(x)))


def _gelu_tanh(x):
    return 0.5 * x * (1.0 + jnp.tanh(math.sqrt(2.0 / math.pi) * (x + 0.044715 * (x * x * x))))


def _l2n(x):
    return x * lax.rsqrt(jnp.sum(x * x, axis=-1, keepdims=True) + EPS)


def _rms_mod(x, g, scale, shift):
    ms = jnp.mean(x * x, axis=-1, keepdims=True)
    y = x * lax.rsqrt(ms + EPS) * g
    return y * (1.0 + scale) + shift


def _pick(n, pref):
    c = pref
    while n % c:
        c //= 2
    return c


class Groups:
    def __init__(self, n_p, l_p, n_s, l_s):
        self.n_p, self.l_p, self.n_s, self.l_s = n_p, l_p, n_s, l_s
        self.t_p = n_p * l_p
        self.t = self.t_p + n_s * l_s

    def row_tile(self, pref):
        return _pick(math.gcd(self.t_p, self.l_s), pref)

    def mod_row(self, i, tm):
        start = i * tm
        return jnp.where(start < self.t_p, 0, 1 + (start - self.t_p) // self.l_s)


def _mod_spec(grp, layer, d, k, tm, ncols=None, two_d=False):
    if two_d:
        per = d // ncols
        return pl.BlockSpec((None, 1, ncols),
                            lambda i, j: (layer * MOD_ROWS + grp.mod_row(i, tm), 0, k * per + j))
    return pl.BlockSpec((None, 1, d), lambda i: (layer * MOD_ROWS + grp.mod_row(i, tm), 0, k))


def _ada_kernel(c_ref, w_ref, b_ref, o_ref):
    s = _silu(c_ref[...]).astype(BF16)
    o_ref[...] = _dot(s, w_ref[...].astype(BF16)) + b_ref[...]


def ada_mod(cond, w_ada, b_ada, tn=1024):
    depth, d, n = w_ada.shape
    tn = _pick(n, tn)
    return pl.pallas_call(
        _ada_kernel,
        grid=(depth, n // tn),
        in_specs=[
            pl.BlockSpec((MOD_ROWS, d), lambda l, j: (0, 0)),
            pl.BlockSpec((None, d, tn), lambda l, j: (l, 0, j)),
            pl.BlockSpec((None, 1, tn), lambda l, j: (l, 0, j)),
        ],
        out_specs=pl.BlockSpec((None, MOD_ROWS, tn), lambda l, j: (l, 0, j)),
        out_shape=jax.ShapeDtypeStruct((depth, MOD_ROWS, n), F32),
        compiler_params=_cparams("arbitrary", "arbitrary"),
        name="ada_mod",
    )(cond, w_ada, b_ada.reshape(depth, 1, n))


def _norm1_kernel(x_ref, g_ref, sh_ref, sc_ref, wg_ref, h_ref, gates_ref):
    h = _rms_mod(x_ref[...], g_ref[...], sc_ref[...], sh_ref[...])
    h_ref[...] = h.astype(BF16)
    gates_ref[...] = _dot3(h, wg_ref[...])


def norm1_gates(x, g, mod, layer, w_gates, grp, tm=ROW_BLK):
    t, d = x.shape
    ng = w_gates.shape[1]
    return pl.pallas_call(
        _norm1_kernel,
        grid=(t // tm,),
        in_specs=[
            pl.BlockSpec((tm, d), lambda i: (i, 0)),
            pl.BlockSpec((1, d), lambda i: (0, 0)),
            _mod_spec(grp, layer, d, 0, tm), _mod_spec(grp, layer, d, 1, tm),
            pl.BlockSpec((d, ng), lambda i: (0, 0)),
        ],
        out_specs=[pl.BlockSpec((tm, d), lambda i: (i, 0)), pl.BlockSpec((tm, ng), lambda i: (i, 0))],
        out_shape=[jax.ShapeDtypeStruct((t, d), BF16), jax.ShapeDtypeStruct((t, ng), F32)],
        compiler_params=_cparams("arbitrary"),
        name="norm1_gates",
    )(x, g.reshape(1, d), mod, mod, w_gates)


def _mm_kernel(x_ref, w_ref, o_ref):
    o_ref[...] = _dot(x_ref[...], w_ref[...])


def matmul(x, w, tm=1024, tn=1024, name="matmul"):
    m, k = x.shape
    n = w.shape[1]
    tm, tn = _pick(m, tm), _pick(n, tn)
    return pl.pallas_call(
        _mm_kernel,
        grid=(m // tm, n // tn),
        in_specs=[pl.BlockSpec((tm, k), lambda i, j: (i, 0)), pl.BlockSpec((k, tn), lambda i, j: (0, j))],
        out_specs=pl.BlockSpec((tm, tn), lambda i, j: (i, j)),
        out_shape=jax.ShapeDtypeStruct((m, n), F32),
        compiler_params=_cparams("arbitrary", "arbitrary"),
        name=name,
    )(x, w)


def _dwconv(x, w):
    l = x.shape[0]
    row = lax.broadcasted_iota(I32, x.shape, 0)
    acc = x * w[2:3, :]
    for j, s in ((0, -2), (1, -1), (3, 1)):
        shifted = pltpu.roll(x, shift=(-s) % l, axis=0)
        valid = (row + s >= 0) & (row + s < l)
        acc = acc + jnp.where(valid, shifted, 0.0) * w[j:j + 1, :]
    return acc


_G_BETA, _G_EG, _G_EKD, _G_EGL, _G_GC = 0, 1, 2, 3, 4


def _unit_tri_inverses(mats):
    n = mats[0][0].shape[0]
    ri = lax.broadcasted_iota(I32, (n, n), 0)
    ci = lax.broadcasted_iota(I32, (n, n), 1)
    ts = [None] * len(mats)
    s = 1
    while s < DN_CHUNK:
        same = (ri // (2 * s)) == (ci // (2 * s))
        for m, (a, lower) in enumerate(mats):
            first, second = (ci, ri) if lower else (ri, ci)
            pair = same & ((first // s) % 2 == 0) & ((second // s) % 2 == 1)
            a_off = jnp.where(pair, a, 0.0)
            if s == 1:
                ts[m] = (ri == ci).astype(F32) - a_off
            else:
                tb = ts[m].astype(BF16)
                ts[m] = ts[m] - _dot(_dot(tb, a_off.astype(BF16)).astype(BF16), tb)
        s *= 2
    return ts


def _dn_kernel(q_ref, k_ref, v_ref, z_ref, gc_ref, gt_ref, cq_ref, ck_ref, cv_ref, ng_ref, s0_ref,
               o_ref, sfin_ref, qs, ks, vs, os_f, os_b, *, nb, dk, heads, hp):
    h0 = pl.program_id(1) * hp
    blk = ROW_BLK
    c_per = blk // DN_CHUNK

    qc = _silu(_dwconv(q_ref[...], cq_ref[...]))
    kc = _silu(_dwconv(k_ref[...], ck_ref[...]))
    vs[...] = _silu(_dwconv(v_ref[...], cv_ref[...]))
    for hh in range(hp):
        ls = slice(hh * LANES, (hh + 1) * LANES)
        qs[:, ls] = _l2n(qc[:, ls]) * (dk ** -0.5)
        ks[:, ls] = _l2n(kc[:, ls])

    ri = lax.broadcasted_iota(I32, (blk, blk), 0)
    ci = lax.broadcasted_iota(I32, (blk, blk), 1)
    same_chunk = (ri // DN_CHUNK) == (ci // DN_CHUNK)

    combos = [(hh, d) for hh in range(hp) for d in range(2)]

    def all_heads(jf, jb, states):
        pre = []
        for hh, d in combos:
            r0 = (jf if d == 0 else jb) * blk
            if not isinstance(r0, int):
                r0 = pl.multiple_of(r0, blk)
            rows = pl.ds(r0, blk)
            ls = slice(hh * LANES, (hh + 1) * LANES)
            q, k, v = qs[rows, ls], ks[rows, ls], vs[rows, ls]
            g = gc_ref[rows, ls]
            base = 5 * d
            beta = g[:, base + _G_BETA:base + _G_BETA + 1]
            eg = g[:, base + _G_EG:base + _G_EG + 1]
            ekd = g[:, base + _G_EKD:base + _G_EKD + 1]
            egl = g[:, base + _G_EGL:base + _G_EGL + 1]
            gcc = g[:, base + _G_GC:base + _G_GC + 1]
            gcr = gt_ref[jf if d == 0 else jb, pl.ds(d * heads + h0 + hh, 1), :]
            incl = same_chunk & ((ri >= ci) if d == 0 else (ri <= ci))
            strict = same_chunk & ((ri > ci) if d == 0 else (ri < ci))
            decay = jnp.where(incl, jnp.exp(jnp.where(incl, gcc - gcr, 0.0)), 0.0)
            kbf = k.astype(BF16)
            kb = k * beta
            a = jnp.where(strict, _dot_nt(kb.astype(BF16), kbf) * decay, 0.0)
            qk = _dot_nt(q.astype(BF16), kbf) * decay
            pre.append(dict(rows=rows, ls=ls, a=a, qk=qk, vb=(v * beta).astype(BF16),
                            kbe=(kb * eg).astype(BF16), q_dec=(q * eg).astype(BF16),
                            k_dec=(k * ekd).astype(BF16), egl=egl))
        ts = _unit_tri_inverses([(p["a"], d == 0) for p, (_, d) in zip(pre, combos)])
        for p, t in zip(pre, ts):
            tb = t.astype(BF16)
            p["u"] = _dot(tb, p["vb"])
            p["w"] = _dot(tb, p["kbe"]).astype(BF16)
        states = list(states)
        outs = [[None] * c_per for _ in combos]
        for step in range(c_per):
            for m, (p, (_, d)) in enumerate(zip(pre, combos)):
                c = step if d == 0 else c_per - 1 - step
                cs = slice(c * DN_CHUNK, (c + 1) * DN_CHUNK)
                sb = states[m].astype(BF16)
                v_new = p["u"][cs, :] - _dot(p["w"][cs, :], sb)
                vnb = v_new.astype(BF16)
                outs[m][c] = _dot(p["q_dec"][cs, :], sb) + _dot(p["qk"][cs, cs].astype(BF16), vnb)
                states[m] = (states[m] * p["egl"][c * DN_CHUNK:c * DN_CHUNK + 1, :]
                             + _dot_tn(p["k_dec"][cs, :], vnb))
        for m, (p, (_, d)) in enumerate(zip(pre, combos)):
            dst = os_f if d == 0 else os_b
            dst[p["rows"], p["ls"]] = jnp.concatenate(outs[m], axis=0)
        return tuple(states)

    states = tuple(s0_ref[d, hh] for hh in range(hp) for d in range(2))
    if nb == 1:
        states = all_heads(0, 0, states)
    else:
        states = lax.fori_loop(0, nb, lambda j, st: all_heads(j, nb - 1 - j, st), states)
    for hh in range(hp):
        for d in range(2):
            sfin_ref[d, hh] = states[2 * hh + d]

    o = os_f[...] + os_b[...]
    z = z_ref[...]
    for hh in range(hp):
        ls = slice(hh * LANES, (hh + 1) * LANES)
        oh = o[:, ls]
        oh = oh * lax.rsqrt(jnp.mean(oh * oh, axis=-1, keepdims=True) + EPS) * ng_ref[...]
        o_ref[:, ls] = (oh * _silu(z[:, ls])).astype(BF16)


def deltanet(proj, gcol, gt, conv_w, norm_g, s0, *, n_seq, l, row0, heads, dk, cols, hp=2):
    nb = l // ROW_BLK
    rb0 = row0 // l
    wl = hp * LANES
    cq, ck, cv, cz = (c // wl for c in cols)
    nh = heads // hp

    def col(cb):
        return pl.BlockSpec((l, wl), lambda b, h: (rb0 + b, cb + h))

    def cw(off):
        return pl.BlockSpec((4, wl), lambda b, h: (0, off + h))

    state_spec = pl.BlockSpec((None, 2, hp, dk, LANES), lambda b, h: (b, 0, h, 0, 0))
    return pl.pallas_call(
        functools.partial(_dn_kernel, nb=nb, dk=dk, heads=heads, hp=hp),
        grid=(n_seq, nh),
        in_specs=[
            col(cq), col(ck), col(cv), col(cz),
            pl.BlockSpec((l, wl), lambda b, h: (rb0 + b, h)),
            pl.BlockSpec((nb, gt.shape[1], ROW_BLK), lambda b, h: (rb0 + b, 0, 0)),
            cw(0), cw(nh), cw(2 * nh),
            pl.BlockSpec((1, LANES), lambda b, h: (0, 0)),
            state_spec,
        ],
        out_specs=[pl.BlockSpec((l, wl), lambda b, h: (b, h)), state_spec],
        out_shape=[jax.ShapeDtypeStruct((n_seq * l, heads * LANES), BF16),
                   jax.ShapeDtypeStruct((n_seq, 2, heads, dk, LANES), F32)],
        scratch_shapes=[pltpu.VMEM((l, wl), F32)] * 5,
        compiler_params=_cparams("arbitrary", "arbitrary"),
        name=f"deltanet_l{l}",
    )(proj, proj, proj, proj, gcol, gt, conv_w, conv_w, conv_w, norm_g.reshape(1, LANES), s0)


def dn_gate_prep(gates, a_log, dt_bias, heads):
    t = gates.shape[0]
    n = t // DN_CHUNK
    beta = jax.nn.sigmoid(gates[:, :2 * heads]).reshape(n, DN_CHUNK, 2, heads)
    g = -jnp.exp(a_log)[None] * jax.nn.softplus(gates[:, 2 * heads:4 * heads].reshape(t, 2, heads) + dt_bias[None])
    g = g.reshape(n, DN_CHUNK, 2, heads)
    gc_f = jnp.cumsum(g[:, :, 0], axis=1)
    gc_b = jnp.flip(jnp.cumsum(jnp.flip(g[:, :, 1], axis=1), axis=1), axis=1)
    gc = jnp.stack([gc_f, gc_b], axis=2)
    glast = jnp.stack([gc_f[:, -1], gc_b[:, 0]], axis=1)[:, None]
    quant = jnp.stack([beta, jnp.exp(gc), jnp.exp(glast - gc),
                       jnp.broadcast_to(jnp.exp(glast), gc.shape), gc], axis=-1)
    quant = jnp.transpose(quant, (0, 1, 3, 2, 4)).reshape(t, heads, 10)
    gcol = jnp.pad(quant, ((0, 0), (0, 0), (0, LANES - 10))).reshape(t, heads * LANES)
    gt = jnp.transpose(gc.reshape(t // ROW_BLK, ROW_BLK, 2 * heads), (0, 2, 1))
    return gcol, gt


def _lru_kernel(x_ref, y_ref, cw_ref, cb_ref, wa_ref, wx_ref, ba_ref, bx_ref, lam_ref, h0_ref,
                o_ref, sfin_ref, a_s, b_s, h_s, *, l, cw):
    xc = _dwconv(x_ref[...], cw_ref[...]) + cb_ref[...]
    for d in range(2):
        sp = _softplus(-lam_ref[d:d + 1, :])
        for s in range(cw // LANES):
            cs = slice(s * LANES, (s + 1) * LANES)
            xs = xc[:, cs]
            xb = xs.astype(BF16)
            r = _sigmoid(_dot(xb, wa_ref[d, s].astype(BF16)) + ba_ref[d:d + 1, cs])
            i = _sigmoid(_dot(xb, wx_ref[d, s].astype(BF16)) + bx_ref[d:d + 1, cs])
            log_a = -LRU_C * r * sp[:, cs]
            a = jnp.exp(log_a)
            a_s[d, :, cs] = a
            b_s[d, :, cs] = jnp.sqrt(jnp.tanh(-log_a) * (a * a + 1.0)) * i * xs

    def fwd(t, h):
        h = a_s[0, pl.ds(t, 1), :] * h + b_s[0, pl.ds(t, 1), :]
        h_s[pl.ds(t, 1), :] = h
        return h

    def bwd(i, h):
        t = l - 1 - i
        h = a_s[1, pl.ds(t, 1), :] * h + b_s[1, pl.ds(t, 1), :]
        h_s[pl.ds(t, 1), :] = h_s[pl.ds(t, 1), :] + h
        return h

    sfin_ref[0:1, :] = lax.fori_loop(0, l, fwd, h0_ref[0:1, :], unroll=8)
    sfin_ref[1:2, :] = lax.fori_loop(0, l, bwd, h0_ref[1:2, :], unroll=8)
    o_ref[...] = (h_s[...] * _gelu_tanh(y_ref[...])).astype(BF16)


def rglru(proj, conv_w, conv_b, wa, wx, ba, bx, lam, h0, *, n_seq, l, row0, cols, cw=512):
    width = conv_w.shape[1]
    cw = _pick(width, cw)
    rb0 = row0 // l
    cx, cy = (c // cw for c in cols)
    nsub = cw // LANES
    vec2 = pl.BlockSpec((2, cw), lambda b, n: (0, n))
    wblk = pl.BlockSpec((2, nsub, LANES, LANES), lambda b, n: (0, n, 0, 0))
    state_spec = pl.BlockSpec((None, 2, cw), lambda b, n: (b, 0, n))
    return pl.pallas_call(
        functools.partial(_lru_kernel, l=l, cw=cw),
        grid=(n_seq, width // cw),
        in_specs=[
            pl.BlockSpec((l, cw), lambda b, n: (rb0 + b, cx + n)),
            pl.BlockSpec((l, cw), lambda b, n: (rb0 + b, cy + n)),
            pl.BlockSpec((4, cw), lambda b, n: (0, n)),
            pl.BlockSpec((1, cw), lambda b, n: (0, n)),
            wblk, wblk, vec2, vec2, vec2, state_spec,
        ],
        out_specs=[pl.BlockSpec((l, cw), lambda b, n: (b, n)), state_spec],
        out_shape=[jax.ShapeDtypeStruct((n_seq * l, width), BF16),
                   jax.ShapeDtypeStruct((n_seq, 2, width), F32)],
        scratch_shapes=[pltpu.VMEM((2, l, cw), F32), pltpu.VMEM((2, l, cw), F32), pltpu.VMEM((l, cw), F32)],
        compiler_params=_cparams("arbitrary", "arbitrary"),
        name=f"rglru_l{l}",
    )(proj, proj, conv_w, conv_b.reshape(1, width), wa, wx, ba, bx, lam, h0)


def _branch_kernel(a_ref, b_ref, wa_ref, wb_ref, ga_ref, gb_ref, o_ref):
    oa = _dot(a_ref[...], wa_ref[...])
    ob = _dot(b_ref[...], wb_ref[...])
    o_ref[...] = (_sigmoid(ga_ref[...]) * oa + _sigmoid(gb_ref[...]) * ob).astype(BF16)


def branch_merge(dn_out, lru_out, wa, wb, proj, cols, tm=1024, tn=512):
    t, ka = dn_out.shape
    kb = lru_out.shape[1]
    d = wa.shape[1]
    tm, tn = _pick(t, tm), _pick(d, tn)
    cga, cgb = (c // tn for c in cols)
    return pl.pallas_call(
        _branch_kernel,
        grid=(t // tm, d // tn),
        in_specs=[
            pl.BlockSpec((tm, ka), lambda i, j: (i, 0)),
            pl.BlockSpec((tm, kb), lambda i, j: (i, 0)),
            pl.BlockSpec((ka, tn), lambda i, j: (0, j)),
            pl.BlockSpec((kb, tn), lambda i, j: (0, j)),
            pl.BlockSpec((tm, tn), lambda i, j: (i, cga + j)),
            pl.BlockSpec((tm, tn), lambda i, j: (i, cgb + j)),
        ],
        out_specs=pl.BlockSpec((tm, tn), lambda i, j: (i, j)),
        out_shape=jax.ShapeDtypeStruct((t, d), BF16),
        compiler_params=_cparams("arbitrary", "arbitrary"),
        name="branch_merge",
    )(dn_out, lru_out, wa, wb, proj, proj)


def _resid_kernel(m_ref, w_ref, x_ref, g_ref, o_ref):
    o_ref[...] = x_ref[...] + g_ref[...] * _dot(m_ref[...], w_ref[...])


def out_proj_residual(m, w, x, mod, layer, grp, tm=1024, tn=512):
    t, k = m.shape
    d = w.shape[1]
    tm, tn = grp.row_tile(tm), _pick(d, tn)
    return pl.pallas_call(
        _resid_kernel,
        grid=(t // tm, d // tn),
        in_specs=[
            pl.BlockSpec((tm, k), lambda i, j: (i, 0)),
            pl.BlockSpec((k, tn), lambda i, j: (0, j)),
            pl.BlockSpec((tm, tn), lambda i, j: (i, j)),
            _mod_spec(grp, layer, d, 2, tm, ncols=tn, two_d=True),
        ],
        out_specs=pl.BlockSpec((tm, tn), lambda i, j: (i, j)),
        out_shape=jax.ShapeDtypeStruct((t, d), F32),
        compiler_params=_cparams("arbitrary", "arbitrary"),
        name="out_proj_residual",
    )(m, w, x, mod)


def _pack_halves(xb):
    n = xb.shape[1] // 2
    lo = pltpu.bitcast(xb[:, :n].astype(F32), jnp.uint32)
    hi = pltpu.bitcast(xb[:, n:].astype(F32), jnp.uint32)
    return (lo >> 16) | (hi & jnp.uint32(0xFFFF0000))


def _unpack_halves(xp):
    lo = pltpu.bitcast(xp << 16, F32).astype(BF16)
    hi = pltpu.bitcast(xp & jnp.uint32(0xFFFF0000), F32).astype(BF16)
    return lo, hi


def _first_argmax(x, idx, n, axis):
    m = jnp.max(x, axis=axis, keepdims=True)
    return jnp.min(jnp.where(x == m, idx, n), axis=axis, keepdims=True)


def _norm2_kernel(x_ref, g_ref, sh_ref, sc_ref, rw_ref, rb_ref, h_ref, hb_ref, idx_ref, w_ref):
    h = _rms_mod(x_ref[...], g_ref[...], sc_ref[...], sh_ref[...])
    hb = h.astype(BF16)
    hb_ref[...] = hb
    h_ref[...] = _pack_halves(hb)
    tm = h.shape[0]
    n_exp = rw_ref.shape[0]
    per = n_exp // N_GROUPS
    scores = _sigmoid(_dot3(rw_ref[...], h, dot=_dot_nt))
    sel = scores + rb_ref[...]
    neg = -jnp.inf
    grp = sel.reshape(N_GROUPS, per, tm)
    sub = lax.broadcasted_iota(I32, grp.shape, 1)
    m1 = jnp.max(grp, axis=1, keepdims=True)
    first = jnp.min(jnp.where(grp == m1, sub, per), axis=1, keepdims=True)
    m2 = jnp.max(jnp.where(sub == first, neg, grp), axis=1, keepdims=True)
    gscore = (m1 + m2).reshape(N_GROUPS, tm)
    gi = lax.broadcasted_iota(I32, gscore.shape, 0)
    gmask = jnp.zeros(gscore.shape, F32)
    for _ in range(TOPK_GROUPS):
        pick = gi == _first_argmax(gscore, gi, N_GROUPS, 0)
        gmask = jnp.where(pick, 1.0, gmask)
        gscore = jnp.where(pick, neg, gscore)
    emask = jnp.broadcast_to(gmask.reshape(N_GROUPS, 1, tm), grp.shape).reshape(n_exp, tm)
    cand = jnp.where(emask > 0.0, sel, neg)
    ei = lax.broadcasted_iota(I32, cand.shape, 0)
    idx_rows, w_rows = [], []
    for _ in range(TOP_K):
        f = _first_argmax(cand, ei, n_exp, 0)
        pick = ei == f
        idx_rows.append(f)
        w_rows.append(jnp.sum(jnp.where(pick, scores, 0.0), axis=0, keepdims=True))
        cand = jnp.where(pick, neg, cand)
    tw = jnp.concatenate(w_rows, axis=0)
    tw = tw / jnp.sum(tw, axis=0, keepdims=True) * ROUTED_SCALE
    pad = 8 - TOP_K
    idx_ref[...] = jnp.concatenate(idx_rows + [jnp.zeros((pad, tm), I32)], axis=0)
    w_ref[...] = jnp.concatenate([tw, jnp.zeros((pad, tm), F32)], axis=0)


def norm2_router(x, g, mod, layer, router_wt, router_bias, grp, tm=ROW_BLK):
    t, d = x.shape
    n_exp = router_wt.shape[0]
    return pl.pallas_call(
        _norm2_kernel,
        grid=(t // tm,),
        in_specs=[
            pl.BlockSpec((tm, d), lambda i: (i, 0)),
            pl.BlockSpec((1, d), lambda i: (0, 0)),
            _mod_spec(grp, layer, d, 3, tm), _mod_spec(grp, layer, d, 4, tm),
            pl.BlockSpec((n_exp, d), lambda i: (0, 0)),
            pl.BlockSpec((n_exp, 1), lambda i: (0, 0)),
        ],
        out_specs=[pl.BlockSpec((tm, d // 2), lambda i: (i, 0)), pl.BlockSpec((tm, d), lambda i: (i, 0)),
                   pl.BlockSpec((8, tm), lambda i: (0, i)), pl.BlockSpec((8, tm), lambda i: (0, i))],
        out_shape=[jax.ShapeDtypeStruct((t, d // 2), jnp.uint32), jax.ShapeDtypeStruct((t, d), BF16),
                   jax.ShapeDtypeStruct((8, t), I32), jax.ShapeDtypeStruct((8, t), F32)],
        compiler_params=_cparams("arbitrary"),
        name="norm2_router",
    )(x, g.reshape(1, d), mod, mod, router_wt, router_bias.reshape(n_exp, 1))


def route_metadata(top_idx, n_exp, tm):
    k, t = top_idx.shape
    n_pairs = k * t
    n_tiles = n_pairs // tm + n_exp
    n_rows = n_tiles * tm
    e_flat = top_idx.reshape(-1)
    onehot = (e_flat[:, None] == jnp.arange(n_exp, dtype=I32)[None, :]).astype(I32)
    csum = jnp.cumsum(onehot, axis=0)
    counts = csum[-1]
    rank = jnp.take_along_axis(csum, e_flat[:, None], axis=1)[:, 0] - 1
    padded = (counts + tm - 1) // tm * tm
    ends = jnp.cumsum(padded)
    off = ends - padded
    pos = off[e_flat] + rank
    pair = jnp.arange(n_pairs, dtype=I32)
    rows = jnp.arange(n_rows, dtype=I32)
    spare = n_pairs + (rows // tm) % 2 * 8 + rows % 8
    row_dst = spare.at[pos].set(pair)
    row_token = jnp.where(row_dst < n_pairs, row_dst % t, 0)
    n_valid = ends[-1] // tm
    start = jnp.arange(n_tiles, dtype=I32) * tm
    te = jnp.minimum(jnp.searchsorted(ends, start, side="right").astype(I32), n_exp - 1)
    last_e = te[jnp.maximum(n_valid - 1, 0)]
    valid = jnp.arange(n_tiles, dtype=I32) < n_valid
    te = jnp.where(valid, te, last_e)
    rows_valid = jnp.where(valid, jnp.clip(counts[te] - (start - off[te]), 0, tm), 0).astype(I32)
    rows_valid = (rows_valid + 7) // 8 * 8
    return dict(tile_expert=te, n_valid=n_valid.reshape(1).astype(I32), rows_valid=rows_valid,
                row_token=row_token.reshape(n_tiles, 1, tm), row_dst=row_dst.reshape(n_tiles, 1, tm),
                n_tiles=n_tiles)


def _moe1_kernel(te_ref, nv_ref, tok_ref, tok_next_ref, h_hbm, wg_ref, wu_ref, o_ref, xbuf, sem, *, tm):
    del te_ref
    i = pl.program_id(0)
    n_valid = nv_ref[0]
    slot = i % 2

    def row_copy(tok, s, r):
        return pltpu.make_async_copy(h_hbm.at[pl.ds(tok, 1), :], xbuf.at[s, pl.ds(r, 1), :], sem.at[s])

    def gather(tref, s):
        def body(r, carry):
            row_copy(tref[0, r], s, r).start()
            return carry
        lax.fori_loop(0, tm, body, 0, unroll=8)

    @pl.when((i == 0) & (n_valid > 0))
    def _():
        gather(tok_ref, 0)

    @pl.when(i + 1 < n_valid)
    def _():
        gather(tok_next_ref, 1 - slot)

    @pl.when(i < n_valid)
    def _():
        pltpu.make_async_copy(h_hbm.at[pl.ds(0, tm), :], xbuf.at[slot], sem.at[slot]).wait()
        x_lo, x_hi = _unpack_halves(xbuf[slot])
        half = x_lo.shape[1]

        def proj(w_ref):
            return (_dot(x_lo, w_ref[:half, :].astype(BF16)) + _dot(x_hi, w_ref[half:, :].astype(BF16)))

        o_ref[...] = (_silu(proj(wg_ref)) * proj(wu_ref)).astype(BF16)

    @pl.when(i >= n_valid)
    def _():
        o_ref[...] = jnp.zeros(o_ref.shape, BF16)


def moe_stage1(h2p, w_gate, w_up, layer, meta, tm=MOE_TM):
    d, de = w_gate.shape[-2:]
    n_tiles = meta["n_tiles"]
    last = n_tiles - 1
    wspec = pl.BlockSpec((None, None, d, de), lambda i, te, nv: (layer, te[i], 0, 0))
    grid_spec = pltpu.PrefetchScalarGridSpec(
        num_scalar_prefetch=2,
        grid=(n_tiles,),
        in_specs=[
            pl.BlockSpec((None, 1, tm), lambda i, te, nv: (i, 0, 0), memory_space=pltpu.SMEM),
            pl.BlockSpec((None, 1, tm), lambda i, te, nv: (jnp.minimum(i + 1, last), 0, 0),
                         memory_space=pltpu.SMEM),
            pl.BlockSpec(memory_space=pl.ANY),
            wspec, wspec,
        ],
        out_specs=pl.BlockSpec((tm, de), lambda i, te, nv: (i, 0)),
        scratch_shapes=[pltpu.VMEM((2, tm, d // 2), jnp.uint32), pltpu.SemaphoreType.DMA((2,))],
    )
    return pl.pallas_call(
        functools.partial(_moe1_kernel, tm=tm),
        grid_spec=grid_spec,
        out_shape=jax.ShapeDtypeStruct((n_tiles * tm, de), BF16),
        compiler_params=_cparams("arbitrary"),
        name="moe_stage1",
    )(meta["tile_expert"], meta["n_valid"], meta["row_token"], meta["row_token"], h2p, w_gate, w_up)


def _moe2_kernel(te_ref, nv_ref, nr_ref, dst_ref, hm_ref, wd_ref, out_hbm, ybuf, sem, *, tm, n_tiles):
    del te_ref
    i = pl.program_id(0)
    n_valid = nv_ref[0]
    slot = i % 2

    def wait_tile(j, s):
        n = pl.multiple_of(nr_ref[j], 8)
        pltpu.make_async_copy(ybuf.at[s, pl.ds(0, n), :], out_hbm.at[pl.ds(0, n), :], sem.at[s]).wait()

    @pl.when(i == 0)
    def _():
        spare = out_hbm.shape[0] - 16
        ybuf[1, 0:16, :] = jnp.zeros((16, ybuf.shape[2]), F32)
        init = pltpu.make_async_copy(ybuf.at[1, pl.ds(0, 16), :], out_hbm.at[pl.ds(spare, 16), :], sem.at[1])
        init.start()
        init.wait()

    @pl.when((i >= 2) & (i - 2 < n_valid))
    def _():
        wait_tile(i - 2, slot)

    @pl.when(i < n_valid)
    def _():
        ybuf[slot] = _dot(hm_ref[...], wd_ref[...].astype(BF16))

        def body(r8, carry):
            for k in range(8):
                r = r8 * 8 + k
                pltpu.make_async_copy(ybuf.at[slot, pl.ds(r, 1), :], out_hbm.at[pl.ds(dst_ref[0, r], 1), :],
                                      sem.at[slot]).start()
            return carry
        lax.fori_loop(0, nr_ref[i] // 8, body, 0)

    @pl.when(i == n_tiles - 1)
    def _():
        @pl.when((i >= 1) & (i - 1 < n_valid))
        def _():
            wait_tile(i - 1, 1 - slot)

        @pl.when(i < n_valid)
        def _():
            wait_tile(i, slot)


def moe_stage2(hmid, w_down, layer, meta, n_out_rows, tm=MOE_TM):
    de = hmid.shape[1]
    d = w_down.shape[-1]
    n_tiles = meta["n_tiles"]
    grid_spec = pltpu.PrefetchScalarGridSpec(
        num_scalar_prefetch=3,
        grid=(n_tiles,),
        in_specs=[
            pl.BlockSpec((None, 1, tm), lambda i, te, nv, nr: (i, 0, 0), memory_space=pltpu.SMEM),
            pl.BlockSpec((tm, de), lambda i, te, nv, nr: (i, 0)),
            pl.BlockSpec((None, None, de, d), lambda i, te, nv, nr: (layer, te[i], 0, 0)),
        ],
        out_specs=pl.BlockSpec(memory_space=pl.ANY),
        scratch_shapes=[pltpu.VMEM((2, tm, d), F32), pltpu.SemaphoreType.DMA((2,))],
    )
    return pl.pallas_call(
        functools.partial(_moe2_kernel, tm=tm, n_tiles=n_tiles),
        grid_spec=grid_spec,
        out_shape=jax.ShapeDtypeStruct((n_out_rows, d), F32),
        compiler_params=_cparams("arbitrary"),
        name="moe_stage2",
    )(meta["tile_expert"], meta["n_valid"], meta["rows_valid"], meta["row_dst"], hmid, w_down)


def _swiglu_in_kernel(x_ref, wg_ref, wu_ref, o_ref):
    x = x_ref[...]
    o_ref[...] = (_silu(_dot(x, wg_ref[...])) * _dot(x, wu_ref[...])).astype(BF16)


def shared_in(hb, wg, wu, tm=1024, tn=512):
    t, d = hb.shape
    n = wg.shape[1]
    tm, tn = _pick(t, tm), _pick(n, tn)
    wspec = pl.BlockSpec((d, tn), lambda i, j: (0, j))
    return pl.pallas_call(
        _swiglu_in_kernel,
        grid=(t // tm, n // tn),
        in_specs=[pl.BlockSpec((tm, d), lambda i, j: (i, 0)), wspec, wspec],
        out_specs=pl.BlockSpec((tm, tn), lambda i, j: (i, j)),
        out_shape=jax.ShapeDtypeStruct((t, n), BF16),
        compiler_params=_cparams("arbitrary", "arbitrary"),
        name="shared_in",
    )(hb, wg, wu)


def _moe_out_kernel(hs_ref, w_ref, x_ref, g_ref, tw_ref, *rest):
    slot_refs, o_ref = rest[:-1], rest[-1]
    tw = tw_ref[...]
    routed = slot_refs[0][...] * tw[:, 0:1]
    for s, r in enumerate(slot_refs[1:], start=1):
        routed = routed + r[...] * tw[:, s:s + 1]
    o_ref[...] = x_ref[...] + g_ref[...] * (routed + _dot(hs_ref[...], w_ref[...]))


def moe_out_residual(hs, w, x, mod, layer, slots, top_w_t, grp, tm=512, tn=1024):
    t, k = hs.shape
    d = w.shape[1]
    tm, tn = grp.row_tile(tm), _pick(d, tn)
    n_slots = TOP_K

    def slot_spec(s):
        return pl.BlockSpec((tm, tn), lambda i, j: (s * (t // tm) + i, j))

    return pl.pallas_call(
        _moe_out_kernel,
        grid=(t // tm, d // tn),
        in_specs=[
            pl.BlockSpec((tm, k), lambda i, j: (i, 0)),
            pl.BlockSpec((k, tn), lambda i, j: (0, j)),
            pl.BlockSpec((tm, tn), lambda i, j: (i, j)),
            _mod_spec(grp, layer, d, 5, tm, ncols=tn, two_d=True),
            pl.BlockSpec((tm, top_w_t.shape[1]), lambda i, j: (i, 0)),
        ] + [slot_spec(s) for s in range(n_slots)],
        out_specs=pl.BlockSpec((tm, tn), lambda i, j: (i, j)),
        out_shape=jax.ShapeDtypeStruct((t, d), F32),
        compiler_params=_cparams("arbitrary", "arbitrary"),
        name="moe_out_residual",
    )(hs, w, x, mod, top_w_t, *([slots] * n_slots))


def _final_norm_kernel(x_ref, g_ref, o_ref):
    x = x_ref[...]
    o_ref[...] = x * lax.rsqrt(jnp.mean(x * x, axis=-1, keepdims=True) + EPS) * g_ref[...]


def final_norm(x, g, tm=512):
    t, d = x.shape
    tm = _pick(t, tm)
    return pl.pallas_call(
        _final_norm_kernel,
        grid=(t // tm,),
        in_specs=[pl.BlockSpec((tm, d), lambda i: (i, 0)), pl.BlockSpec((1, d), lambda i: (0, 0))],
        out_specs=pl.BlockSpec((tm, d), lambda i: (i, 0)),
        out_shape=jax.ShapeDtypeStruct((t, d), F32),
        compiler_params=_cparams("arbitrary"),
        name="final_norm",
    )(x, g.reshape(1, d))


def _grid_sincos(n_tokens, dim):
    rows = n_tokens // GRID_W
    quarter = dim // 4
    omega = 1.0 / (10000.0 ** (jnp.arange(quarter, dtype=F32) / quarter))
    r = jnp.broadcast_to(jnp.arange(rows, dtype=F32)[:, None, None] * omega, (rows, GRID_W, quarter))
    cl = jnp.broadcast_to(jnp.arange(GRID_W, dtype=F32)[None, :, None] * omega, (rows, GRID_W, quarter))
    emb = jnp.concatenate([jnp.sin(r), jnp.cos(r), jnp.sin(cl), jnp.cos(cl)], axis=-1)
    return emb.reshape(n_tokens, dim)


def kernel(x_prompt, x_sample, state_delta, state_lru, c, c_ctx, w_ada, b_ada, norm1_g, w_in, dn_conv_w,
           dn_a_log, dn_dt_bias, dn_norm_g, w_branch_a, lru_conv_w, lru_conv_b, lru_wa, lru_ba, lru_wx,
           lru_bx, lru_lambda, w_branch_b, w_out, norm2_g, router_w, router_bias, w_exp_gate, w_exp_up,
           w_exp_down, w_sh_gate, w_sh_up, w_sh_down, final_norm_g):
    n_p, l_p, d = x_prompt.shape
    n_s, l_s, _ = x_sample.shape
    depth = w_ada.shape[0]
    heads, dk = state_delta.shape[3], state_delta.shape[4]
    qk_w = heads * dk
    v_w = heads * state_delta.shape[5]
    lru_w = state_lru.shape[-1]
    n_exp = router_w.shape[-1]
    grp = Groups(n_p, l_p, n_s, l_s)
    t = grp.t
    assert l_p % ROW_BLK == 0 and l_s % ROW_BLK == 0 and dk == LANES and v_w == qk_w

    x = jnp.concatenate([x_prompt.reshape(n_p * l_p, d),
                         (x_sample + _grid_sincos(l_s, d)[None]).reshape(n_s * l_s, d)], axis=0)
    cond = jnp.concatenate([c_ctx[None], c, jnp.zeros((MOD_ROWS - 1 - n_s, d), F32)], axis=0)
    mod = ada_mod(cond, w_ada, b_ada).reshape(depth * MOD_ROWS, 1, 6 * d)

    g0 = 2 * qk_w + 2 * v_w
    g1 = g0 + 4 * heads
    col_q, col_k, col_v, col_z = 0, qk_w, 2 * qk_w, 2 * qk_w + v_w
    col_lx, col_ly = g0, g0 + lru_w
    col_ga, col_gb = g0 + 2 * lru_w, g0 + 2 * lru_w + d

    zero_delta = jnp.zeros((n_p, 2, heads, dk, LANES), F32)
    zero_lru = jnp.zeros((n_p, 2, lru_w), F32)
    new_delta, new_lru = [], []
    for l in range(depth):
        w_main = jnp.concatenate([w_in[l][:, :g0], w_in[l][:, g1:]], axis=1).astype(BF16)
        w_gates = jnp.pad(w_in[l][:, g0:g1], ((0, 0), (0, LANES - 4 * heads)))

        h, gates = norm1_gates(x, norm1_g[l], mod, l, w_gates, grp)
        proj = matmul(h, w_main, name="in_proj")
        gcol, gt = dn_gate_prep(gates, dn_a_log[l], dn_dt_bias[l], heads)

        dn_kw = dict(heads=heads, dk=dk, cols=(col_q, col_k, col_v, col_z))
        dn_p, s_delta = deltanet(proj, gcol, gt, dn_conv_w[l], dn_norm_g[l], zero_delta,
                                 n_seq=n_p, l=l_p, row0=0, **dn_kw)
        dn_s, _ = deltanet(proj, gcol, gt, dn_conv_w[l], dn_norm_g[l], state_delta[:, l],
                           n_seq=n_s, l=l_s, row0=grp.t_p, **dn_kw)
        lru_args = (proj, lru_conv_w[l], lru_conv_b[l], lru_wa[l], lru_wx[l], lru_ba[l], lru_bx[l],
                    lru_lambda[l])
        lru_p, s_lru = rglru(*lru_args, zero_lru, n_seq=n_p, l=l_p, row0=0, cols=(col_lx, col_ly))
        lru_s, _ = rglru(*lru_args, state_lru[:, l], n_seq=n_s, l=l_s, row0=grp.t_p, cols=(col_lx, col_ly))
        new_delta.append(s_delta)
        new_lru.append(s_lru)

        merged = branch_merge(jnp.concatenate([dn_p, dn_s], axis=0), jnp.concatenate([lru_p, lru_s], axis=0),
                              w_branch_a[l].astype(BF16), w_branch_b[l].astype(BF16), proj, (col_ga, col_gb))
        x = out_proj_residual(merged, w_out[l].astype(BF16), x, mod, l, grp)

        h2, h2b, top_idx, top_w = norm2_router(x, norm2_g[l], mod, l, router_w[l].T, router_bias[l], grp)
        meta = route_metadata(top_idx[:TOP_K], n_exp, MOE_TM)
        hmid = moe_stage1(h2, w_exp_gate, w_exp_up, l, meta)
        slots = moe_stage2(hmid, w_exp_down, l, meta, TOP_K * t + 16)
        hs = shared_in(h2b, w_sh_gate[l].astype(BF16), w_sh_up[l].astype(BF16))
        x = moe_out_residual(hs, w_sh_down[l].astype(BF16), x, mod, l, slots, top_w.T, grp)

    y = final_norm(x, final_norm_g)
    y_prompt = y[:grp.t_p].reshape(n_p, l_p, d)
    y_sample = y[grp.t_p:].reshape(n_s, l_s, d)
    return (y_prompt, y_sample, jnp.stack(new_delta, axis=1), jnp.stack(new_lru, axis=1))
```

```python
import functools
import math

import jax
import jax.numpy as jnp
from jax import lax
from jax.experimental import pallas as pl
from jax.experimental.pallas import tpu as pltpu

F32 = jnp.float32
BF16 = jnp.bfloat16
I32 = jnp.int32

EPS = 1e-6
GRID_W = 64
DN_CHUNK = 64
LRU_C = 8.0
TOP_K = 6
N_GROUPS = 8
TOPK_GROUPS = 4
ROUTED_SCALE = 2.5

V7X_VMEM_LIMIT_BYTES = 56 * 1024 * 1024
LANES = 128
ROW_BLK = 256
MOE_TM = 256
MOD_ROWS = 8
LRU_GROUP = 8


def _cparams(*sem):
    return pltpu.CompilerParams(dimension_semantics=sem, vmem_limit_bytes=V7X_VMEM_LIMIT_BYTES)


def _dot(a, b):
    return jnp.dot(a, b, preferred_element_type=F32)


def _dot_nt(a, b):
    return lax.dot_general(a, b, (((1,), (1,)), ((), ())), preferred_element_type=F32)


def _dot_tn(a, b):
    return lax.dot_general(a, b, (((0,), (0,)), ((), ())), preferred_element_type=F32)


def _split(a):
    hi = a.astype(BF16)
    lo = (a - hi.astype(F32)).astype(BF16)
    return hi, lo


def _dot3(a, b, dot=_dot):
    ah, al = _split(a)
    bh, bl = _split(b)
    return dot(ah, bh) + (dot(ah, bl) + dot(al, bh))


def _sigmoid(x):
    return 1.0 / (1.0 + jnp.exp(-x))


def _silu(x):
    return x * _sigmoid(x)


def _softplus(x):
    return jnp.maximum(x, 0.0) + jnp.log1p(jnp.exp(-jnp.abs(x)))


def _gelu_tanh(x):
    return 0.5 * x * (1.0 + jnp.tanh(math.sqrt(2.0 / math.pi) * (x + 0.044715 * (x * x * x))))


def _l2n(x):
    return x * lax.rsqrt(jnp.sum(x * x, axis=-1, keepdims=True) + EPS)


def _rms_mod(x, g, scale, shift):
    ms = jnp.mean(x * x, axis=-1, keepdims=True)
    y = x * lax.rsqrt(ms + EPS) * g
    return y * (1.0 + scale) + shift


def _pick(n, pref):
    c = pref
    while n % c:
        c //= 2
    return c


class Groups:
    def __init__(self, n_p, l_p, n_s, l_s):
        self.n_p, self.l_p, self.n_s, self.l_s = n_p, l_p, n_s, l_s
        self.t_p = n_p * l_p
        self.t = self.t_p + n_s * l_s

    def row_tile(self, pref):
        return _pick(math.gcd(self.t_p, self.l_s), pref)

    def mod_row(self, i, tm):
        start = i * tm
        return jnp.where(start < self.t_p, 0, 1 + (start - self.t_p) // self.l_s)


def _mod_spec(grp, layer, d, k, tm, ncols=None, two_d=False):
    if two_d:
        per = d // ncols
        return pl.BlockSpec((None, 1, ncols),
                            lambda i, j: (layer * MOD_ROWS + grp.mod_row(i, tm), 0, k * per + j))
    return pl.BlockSpec((None, 1, d), lambda i: (layer * MOD_ROWS + grp.mod_row(i, tm), 0, k))


def _ada_kernel(ct_ref, w_ref, b_ref, o_ref, *, n_rows):
    s = _silu(ct_ref[...])
    w = w_ref[...]
    rows = [jnp.sum(w * s[:, r:r + 1], axis=0, keepdims=True) for r in range(n_rows)]
    rows.append(jnp.zeros((MOD_ROWS - n_rows, w.shape[1]), F32))
    o_ref[...] = jnp.concatenate(rows, axis=0) + b_ref[...]


def ada_mod(cond_t, n_rows, w_ada, b_ada, tn=1024):
    depth, d, n = w_ada.shape
    tn = _pick(n, tn)
    return pl.pallas_call(
        functools.partial(_ada_kernel, n_rows=n_rows),
        grid=(depth, n // tn),
        in_specs=[
            pl.BlockSpec((d, MOD_ROWS), lambda l, j: (0, 0)),
            pl.BlockSpec((None, d, tn), lambda l, j: (l, 0, j)),
            pl.BlockSpec((None, 1, tn), lambda l, j: (l, 0, j)),
        ],
        out_specs=pl.BlockSpec((None, MOD_ROWS, tn), lambda l, j: (l, 0, j)),
        out_shape=jax.ShapeDtypeStruct((depth, MOD_ROWS, n), F32),
        compiler_params=_cparams("arbitrary", "arbitrary"),
        name="ada_mod",
    )(cond_t, w_ada, b_ada.reshape(depth, 1, n))


def _norm1_kernel(x_ref, g_ref, sh_ref, sc_ref, wg_ref, h_ref, gates_ref):
    h = _rms_mod(x_ref[...], g_ref[...], sc_ref[...], sh_ref[...])
    h_ref[...] = h.astype(BF16)
    gates_ref[...] = _dot3(h, wg_ref[...])


def norm1_gates(x, g, mod, layer, w_gates, grp, tm=ROW_BLK):
    t, d = x.shape
    ng = w_gates.shape[1]
    return pl.pallas_call(
        _norm1_kernel,
        grid=(t // tm,),
        in_specs=[
            pl.BlockSpec((tm, d), lambda i: (i, 0)),
            pl.BlockSpec((1, d), lambda i: (0, 0)),
            _mod_spec(grp, layer, d, 0, tm), _mod_spec(grp, layer, d, 1, tm),
            pl.BlockSpec((d, ng), lambda i: (0, 0)),
        ],
        out_specs=[pl.BlockSpec((tm, d), lambda i: (i, 0)), pl.BlockSpec((tm, ng), lambda i: (i, 0))],
        out_shape=[jax.ShapeDtypeStruct((t, d), BF16), jax.ShapeDtypeStruct((t, ng), F32)],
        compiler_params=_cparams("arbitrary"),
        name="norm1_gates",
    )(x, g.reshape(1, d), mod, mod, w_gates)


def _mm_kernel(x_ref, w_ref, o_ref):
    o_ref[...] = _dot(x_ref[...], w_ref[...])


def matmul(x, w, tm=1024, tn=1024, name="matmul"):
    m, k = x.shape
    n = w.shape[1]
    tm, tn = _pick(m, tm), _pick(n, tn)
    return pl.pallas_call(
        _mm_kernel,
        grid=(m // tm, n // tn),
        in_specs=[pl.BlockSpec((tm, k), lambda i, j: (i, 0)), pl.BlockSpec((k, tn), lambda i, j: (0, j))],
        out_specs=pl.BlockSpec((tm, tn), lambda i, j: (i, j)),
        out_shape=jax.ShapeDtypeStruct((m, n), F32),
        compiler_params=_cparams("arbitrary", "arbitrary"),
        name=name,
    )(x, w)


def _dwconv(x, w):
    l = x.shape[0]
    row = lax.broadcasted_iota(I32, x.shape, 0)
    acc = x * w[2:3, :]
    for j, s in ((0, -2), (1, -1), (3, 1)):
        shifted = pltpu.roll(x, shift=(-s) % l, axis=0)
        valid = (row + s >= 0) & (row + s < l)
        acc = acc + jnp.where(valid, shifted, 0.0) * w[j:j + 1, :]
    return acc


_G_BETA, _G_EG, _G_EKD, _G_EGL, _G_GC = 0, 1, 2, 3, 4


def _unit_tri_inverses(mats):
    n = mats[0][0].shape[0]
    ri = lax.broadcasted_iota(I32, (n, n), 0)
    ci = lax.broadcasted_iota(I32, (n, n), 1)
    ts = [None] * len(mats)
    s = 1
    while s < DN_CHUNK:
        same = (ri // (2 * s)) == (ci // (2 * s))
        for m, (a, lower) in enumerate(mats):
            first, second = (ci, ri) if lower else (ri, ci)
            pair = same & ((first // s) % 2 == 0) & ((second // s) % 2 == 1)
            a_off = jnp.where(pair, a, 0.0)
            if s == 1:
                ts[m] = (ri == ci).astype(F32) - a_off
            else:
                tb = ts[m].astype(BF16)
                ts[m] = ts[m] - _dot(_dot(tb, a_off.astype(BF16)).astype(BF16), tb)
        s *= 2
    return ts


def _dn_kernel(q_ref, k_ref, v_ref, z_ref, gc_ref, gt_ref, cq_ref, ck_ref, cv_ref, ng_ref, s0_ref,
               o_ref, sfin_ref, qs, ks, vs, os_f, os_b, *, nb, dk, heads, hp):
    h0 = pl.program_id(1) * hp
    blk = ROW_BLK
    c_per = blk // DN_CHUNK

    qc = _silu(_dwconv(q_ref[...], cq_ref[...]))
    kc = _silu(_dwconv(k_ref[...], ck_ref[...]))
    vs[...] = _silu(_dwconv(v_ref[...], cv_ref[...]))
    for hh in range(hp):
        ls = slice(hh * LANES, (hh + 1) * LANES)
        qs[:, ls] = _l2n(qc[:, ls]) * (dk ** -0.5)
        ks[:, ls] = _l2n(kc[:, ls])

    ri = lax.broadcasted_iota(I32, (blk, blk), 0)
    ci = lax.broadcasted_iota(I32, (blk, blk), 1)
    same_chunk = (ri // DN_CHUNK) == (ci // DN_CHUNK)

    combos = [(hh, d) for hh in range(hp) for d in range(2)]

    def all_heads(jf, jb, states):
        pre = []
        for hh, d in combos:
            r0 = (jf if d == 0 else jb) * blk
            if not isinstance(r0, int):
                r0 = pl.multiple_of(r0, blk)
            rows = pl.ds(r0, blk)
            ls = slice(hh * LANES, (hh + 1) * LANES)
            q, k, v = qs[rows, ls], ks[rows, ls], vs[rows, ls]
            g = gc_ref[rows, ls]
            base = 5 * d
            beta = g[:, base + _G_BETA:base + _G_BETA + 1]
            eg = g[:, base + _G_EG:base + _G_EG + 1]
            ekd = g[:, base + _G_EKD:base + _G_EKD + 1]
            egl = g[:, base + _G_EGL:base + _G_EGL + 1]
            gcc = g[:, base + _G_GC:base + _G_GC + 1]
            gcr = gt_ref[jf if d == 0 else jb, pl.ds(d * heads + h0 + hh, 1), :]
            incl = same_chunk & ((ri >= ci) if d == 0 else (ri <= ci))
            strict = same_chunk & ((ri > ci) if d == 0 else (ri < ci))
            decay = jnp.where(incl, jnp.exp(jnp.where(incl, gcc - gcr, 0.0)), 0.0)
            kbf = k.astype(BF16)
            kb = k * beta
            a = jnp.where(strict, _dot_nt(kb.astype(BF16), kbf) * decay, 0.0)
            qk = _dot_nt(q.astype(BF16), kbf) * decay
            pre.append(dict(rows=rows, ls=ls, a=a, qk=qk, vb=(v * beta).astype(BF16),
                            kbe=(kb * eg).astype(BF16), q_dec=(q * eg).astype(BF16),
                            k_dec=(k * ekd).astype(BF16), egl=egl))
        ts = _unit_tri_inverses([(p["a"], d == 0) for p, (_, d) in zip(pre, combos)])
        for p, t in zip(pre, ts):
            tb = t.astype(BF16)
            p["u"] = _dot(tb, p["vb"])
            p["w"] = _dot(tb, p["kbe"]).astype(BF16)
        states = list(states)
        outs = [[None] * c_per for _ in combos]
        for step in range(c_per):
            for m, (p, (_, d)) in enumerate(zip(pre, combos)):
                c = step if d == 0 else c_per - 1 - step
                cs = slice(c * DN_CHUNK, (c + 1) * DN_CHUNK)
                sb = states[m].astype(BF16)
                v_new = p["u"][cs, :] - _dot(p["w"][cs, :], sb)
                vnb = v_new.astype(BF16)
                outs[m][c] = _dot(p["q_dec"][cs, :], sb) + _dot(p["qk"][cs, cs].astype(BF16), vnb)
                states[m] = (states[m] * p["egl"][c * DN_CHUNK:c * DN_CHUNK + 1, :]
                             + _dot_tn(p["k_dec"][cs, :], vnb))
        for m, (p, (_, d)) in enumerate(zip(pre, combos)):
            dst = os_f if d == 0 else os_b
            dst[p["rows"], p["ls"]] = jnp.concatenate(outs[m], axis=0)
        return tuple(states)

    states = tuple(s0_ref[d, hh] for hh in range(hp) for d in range(2))
    if nb == 1:
        states = all_heads(0, 0, states)
    else:
        states = lax.fori_loop(0, nb, lambda j, st: all_heads(j, nb - 1 - j, st), states)
    for hh in range(hp):
        for d in range(2):
            sfin_ref[d, hh] = states[2 * hh + d]

    o = os_f[...] + os_b[...]
    z = z_ref[...]
    for hh in range(hp):
        ls = slice(hh * LANES, (hh + 1) * LANES)
        oh = o[:, ls]
        oh = oh * lax.rsqrt(jnp.mean(oh * oh, axis=-1, keepdims=True) + EPS) * ng_ref[...]
        o_ref[:, ls] = (oh * _silu(z[:, ls])).astype(BF16)


def deltanet(proj, gcol, gt, conv_w, norm_g, s0, *, n_seq, l, row0, heads, dk, cols, hp=2):
    nb = l // ROW_BLK
    rb0 = row0 // l
    wl = hp * LANES
    cq, ck, cv, cz = (c // wl for c in cols)
    nh = heads // hp

    def col(cb):
        return pl.BlockSpec((l, wl), lambda b, h: (rb0 + b, cb + h))

    def cw(off):
        return pl.BlockSpec((4, wl), lambda b, h: (0, off + h))

    state_spec = pl.BlockSpec((None, 2, hp, dk, LANES), lambda b, h: (b, 0, h, 0, 0))
    return pl.pallas_call(
        functools.partial(_dn_kernel, nb=nb, dk=dk, heads=heads, hp=hp),
        grid=(n_seq, nh),
        in_specs=[
            col(cq), col(ck), col(cv), col(cz),
            pl.BlockSpec((l, wl), lambda b, h: (rb0 + b, h)),
            pl.BlockSpec((nb, gt.shape[1], ROW_BLK), lambda b, h: (rb0 + b, 0, 0)),
            cw(0), cw(nh), cw(2 * nh),
            pl.BlockSpec((1, LANES), lambda b, h: (0, 0)),
            state_spec,
        ],
        out_specs=[pl.BlockSpec((l, wl), lambda b, h: (b, h)), state_spec],
        out_shape=[jax.ShapeDtypeStruct((n_seq * l, heads * LANES), BF16),
                   jax.ShapeDtypeStruct((n_seq, 2, heads, dk, LANES), F32)],
        scratch_shapes=[pltpu.VMEM((l, wl), F32)] * 5,
        compiler_params=_cparams("arbitrary", "arbitrary"),
        name=f"deltanet_l{l}",
    )(proj, proj, proj, proj, gcol, gt, conv_w, conv_w, conv_w, norm_g.reshape(1, LANES), s0)


def dn_gate_prep(gates, a_log, dt_bias, heads):
    t = gates.shape[0]
    n = t // DN_CHUNK
    beta = jax.nn.sigmoid(gates[:, :2 * heads]).reshape(n, DN_CHUNK, 2, heads)
    g = -jnp.exp(a_log)[None] * jax.nn.softplus(gates[:, 2 * heads:4 * heads].reshape(t, 2, heads) + dt_bias[None])
    g = g.reshape(n, DN_CHUNK, 2, heads)
    gc_f = jnp.cumsum(g[:, :, 0], axis=1)
    gc_b = jnp.flip(jnp.cumsum(jnp.flip(g[:, :, 1], axis=1), axis=1), axis=1)
    gc = jnp.stack([gc_f, gc_b], axis=2)
    glast = jnp.stack([gc_f[:, -1], gc_b[:, 0]], axis=1)[:, None]
    quant = jnp.stack([beta, jnp.exp(gc), jnp.exp(glast - gc),
                       jnp.broadcast_to(jnp.exp(glast), gc.shape), gc], axis=-1)
    quant = jnp.transpose(quant, (0, 1, 3, 2, 4)).reshape(t, heads, 10)
    gcol = jnp.pad(quant, ((0, 0), (0, 0), (0, LANES - 10))).reshape(t, heads * LANES)
    gt = jnp.transpose(gc.reshape(t // ROW_BLK, ROW_BLK, 2 * heads), (0, 2, 1))
    return gcol, gt


def _lru_kernel(x_ref, y_ref, cw_ref, cb_ref, wa_ref, wx_ref, ba_ref, bx_ref, lam_ref, h0_ref,
                o_ref, sfin_ref, a_s, b_s, hin_s, *, l, cw):
    xc = _dwconv(x_ref[...], cw_ref[...]) + cb_ref[...]
    for d in range(2):
        sp = _softplus(-lam_ref[d:d + 1, :])
        for s in range(cw // LANES):
            cs = slice(s * LANES, (s + 1) * LANES)
            xs = xc[:, cs]
            xb = xs.astype(BF16)
            r = _sigmoid(_dot(xb, wa_ref[d, s].astype(BF16)) + ba_ref[d:d + 1, cs])
            i = _sigmoid(_dot(xb, wx_ref[d, s].astype(BF16)) + bx_ref[d:d + 1, cs])
            log_a = -LRU_C * r * sp[:, cs]
            a = jnp.exp(log_a)
            b = jnp.sqrt(jnp.tanh(-log_a) * (a * a + 1.0)) * i * xs
            sub = lax.broadcasted_iota(I32, a.shape, 0) % LRU_GROUP
            for sh in (1, 2, 4):
                shift = sh if d == 0 else l - sh
                ok = (sub >= sh) if d == 0 else (sub < LRU_GROUP - sh)
                a_n = pltpu.roll(a, shift=shift, axis=0)
                b_n = pltpu.roll(b, shift=shift, axis=0)
                b = jnp.where(ok, a * b_n + b, b)
                a = jnp.where(ok, a * a_n, a)
            a_s[d, :, cs] = a
            b_s[d, :, cs] = b

    n_grp = l // LRU_GROUP

    def carry_groups(d):
        def body(i, h):
            g = i if d == 0 else n_grp - 1 - i
            r0 = pl.multiple_of(g * LRU_GROUP, LRU_GROUP)
            hin_s[d, pl.ds(r0, LRU_GROUP), :] = jnp.broadcast_to(h, (LRU_GROUP, cw))
            last = r0 + (LRU_GROUP - 1 if d == 0 else 0)
            return a_s[d, pl.ds(last, 1), :] * h + b_s[d, pl.ds(last, 1), :]
        return lax.fori_loop(0, n_grp, body, h0_ref[d:d + 1, :], unroll=4)

    sfin_ref[0:1, :] = carry_groups(0)
    sfin_ref[1:2, :] = carry_groups(1)
    h = (a_s[0] * hin_s[0] + b_s[0]) + (a_s[1] * hin_s[1] + b_s[1])
    o_ref[...] = (h * _gelu_tanh(y_ref[...])).astype(BF16)


def rglru(proj, conv_w, conv_b, wa, wx, ba, bx, lam, h0, *, n_seq, l, row0, cols, cw=512):
    width = conv_w.shape[1]
    cw = _pick(width, cw)
    rb0 = row0 // l
    cx, cy = (c // cw for c in cols)
    nsub = cw // LANES
    vec2 = pl.BlockSpec((2, cw), lambda b, n: (0, n))
    wblk = pl.BlockSpec((2, nsub, LANES, LANES), lambda b, n: (0, n, 0, 0))
    state_spec = pl.BlockSpec((None, 2, cw), lambda b, n: (b, 0, n))
    return pl.pallas_call(
        functools.partial(_lru_kernel, l=l, cw=cw),
        grid=(n_seq, width // cw),
        in_specs=[
            pl.BlockSpec((l, cw), lambda b, n: (rb0 + b, cx + n)),
            pl.BlockSpec((l, cw), lambda b, n: (rb0 + b, cy + n)),
            pl.BlockSpec((4, cw), lambda b, n: (0, n)),
            pl.BlockSpec((1, cw), lambda b, n: (0, n)),
            wblk, wblk, vec2, vec2, vec2, state_spec,
        ],
        out_specs=[pl.BlockSpec((l, cw), lambda b, n: (b, n)), state_spec],
        out_shape=[jax.ShapeDtypeStruct((n_seq * l, width), BF16),
                   jax.ShapeDtypeStruct((n_seq, 2, width), F32)],
        scratch_shapes=[pltpu.VMEM((2, l, cw), F32)] * 3,
        compiler_params=_cparams("arbitrary", "arbitrary"),
        name=f"rglru_l{l}",
    )(proj, proj, conv_w, conv_b.reshape(1, width), wa, wx, ba, bx, lam, h0)


def _branch_kernel(ap_ref, as_ref, bp_ref, bs_ref, wa_ref, wb_ref, ga_ref, gb_ref, o_ref, *, n_p_tiles):
    def emit(a_ref, b_ref):
        oa = _dot(a_ref[...], wa_ref[...])
        ob = _dot(b_ref[...], wb_ref[...])
        o_ref[...] = (_sigmoid(ga_ref[...]) * oa + _sigmoid(gb_ref[...]) * ob).astype(BF16)

    @pl.when(pl.program_id(0) < n_p_tiles)
    def _():
        emit(ap_ref, bp_ref)

    @pl.when(pl.program_id(0) >= n_p_tiles)
    def _():
        emit(as_ref, bs_ref)


def _two_group_specs(shape, n_p_tiles):
    return (pl.BlockSpec(shape, lambda i, *_: (jnp.minimum(i, n_p_tiles - 1), 0)),
            pl.BlockSpec(shape, lambda i, *_: (jnp.maximum(i - n_p_tiles, 0), 0)))


def branch_merge(dn_p, dn_s, lru_p, lru_s, wa, wb, proj, cols, grp, tm=1024, tn=512):
    ka, kb = dn_p.shape[1], lru_p.shape[1]
    t = grp.t
    d = wa.shape[1]
    tm, tn = grp.row_tile(tm), _pick(d, tn)
    n_p_tiles = grp.t_p // tm
    cga, cgb = (c // tn for c in cols)
    return pl.pallas_call(
        functools.partial(_branch_kernel, n_p_tiles=n_p_tiles),
        grid=(t // tm, d // tn),
        in_specs=[
            *_two_group_specs((tm, ka), n_p_tiles),
            *_two_group_specs((tm, kb), n_p_tiles),
            pl.BlockSpec((ka, tn), lambda i, j: (0, j)),
            pl.BlockSpec((kb, tn), lambda i, j: (0, j)),
            pl.BlockSpec((tm, tn), lambda i, j: (i, cga + j)),
            pl.BlockSpec((tm, tn), lambda i, j: (i, cgb + j)),
        ],
        out_specs=pl.BlockSpec((tm, tn), lambda i, j: (i, j)),
        out_shape=jax.ShapeDtypeStruct((t, d), BF16),
        compiler_params=_cparams("arbitrary", "arbitrary"),
        name="branch_merge",
    )(dn_p, dn_s, lru_p, lru_s, wa, wb, proj, proj)


def _resid_kernel(m_ref, w_ref, x_ref, g_ref, o_ref):
    o_ref[...] = x_ref[...] + g_ref[...] * _dot(m_ref[...], w_ref[...])


def out_proj_residual(m, w, x, mod, layer, grp, tm=1024, tn=512):
    t, k = m.shape
    d = w.shape[1]
    tm, tn = grp.row_tile(tm), _pick(d, tn)
    return pl.pallas_call(
        _resid_kernel,
        grid=(t // tm, d // tn),
        in_specs=[
            pl.BlockSpec((tm, k), lambda i, j: (i, 0)),
            pl.BlockSpec((k, tn), lambda i, j: (0, j)),
            pl.BlockSpec((tm, tn), lambda i, j: (i, j)),
            _mod_spec(grp, layer, d, 2, tm, ncols=tn, two_d=True),
        ],
        out_specs=pl.BlockSpec((tm, tn), lambda i, j: (i, j)),
        out_shape=jax.ShapeDtypeStruct((t, d), F32),
        compiler_params=_cparams("arbitrary", "arbitrary"),
        name="out_proj_residual",
    )(m, w, x, mod)


def _pack_halves(xb):
    n = xb.shape[1] // 2
    lo = pltpu.bitcast(xb[:, :n].astype(F32), jnp.uint32)
    hi = pltpu.bitcast(xb[:, n:].astype(F32), jnp.uint32)
    return (lo >> 16) | (hi & jnp.uint32(0xFFFF0000))


def _unpack_halves(xp):
    lo = pltpu.bitcast(xp << 16, F32).astype(BF16)
    hi = pltpu.bitcast(xp & jnp.uint32(0xFFFF0000), F32).astype(BF16)
    return lo, hi


def _first_argmax(x, idx, n, axis):
    m = jnp.max(x, axis=axis, keepdims=True)
    return jnp.min(jnp.where(x == m, idx, n), axis=axis, keepdims=True)


def _norm2_kernel(x_ref, g_ref, sh_ref, sc_ref, rw_ref, rb_ref, h_ref, hb_ref, idx_ref, w_ref):
    h = _rms_mod(x_ref[...], g_ref[...], sc_ref[...], sh_ref[...])
    hb = h.astype(BF16)
    hb_ref[...] = hb
    h_ref[...] = _pack_halves(hb)
    tm = h.shape[0]
    n_exp = rw_ref.shape[0]
    per = n_exp // N_GROUPS
    scores = _sigmoid(_dot3(rw_ref[...], h, dot=_dot_nt))
    sel = scores + rb_ref[...]
    neg = -jnp.inf
    grp = sel.reshape(N_GROUPS, per, tm)
    sub = lax.broadcasted_iota(I32, grp.shape, 1)
    m1 = jnp.max(grp, axis=1, keepdims=True)
    first = jnp.min(jnp.where(grp == m1, sub, per), axis=1, keepdims=True)
    m2 = jnp.max(jnp.where(sub == first, neg, grp), axis=1, keepdims=True)
    gscore = (m1 + m2).reshape(N_GROUPS, tm)
    gi = lax.broadcasted_iota(I32, gscore.shape, 0)
    gmask = jnp.zeros(gscore.shape, F32)
    for _ in range(TOPK_GROUPS):
        pick = gi == _first_argmax(gscore, gi, N_GROUPS, 0)
        gmask = jnp.where(pick, 1.0, gmask)
        gscore = jnp.where(pick, neg, gscore)
    emask = jnp.broadcast_to(gmask.reshape(N_GROUPS, 1, tm), grp.shape).reshape(n_exp, tm)
    cand = jnp.where(emask > 0.0, sel, neg)
    ei = lax.broadcasted_iota(I32, cand.shape, 0)
    idx_rows, w_rows = [], []
    for _ in range(TOP_K):
        f = _first_argmax(cand, ei, n_exp, 0)
        pick = ei == f
        idx_rows.append(f)
        w_rows.append(jnp.sum(jnp.where(pick, scores, 0.0), axis=0, keepdims=True))
        cand = jnp.where(pick, neg, cand)
    tw = jnp.concatenate(w_rows, axis=0)
    tw = tw / jnp.sum(tw, axis=0, keepdims=True) * ROUTED_SCALE
    pad = 8 - TOP_K
    idx_ref[...] = jnp.concatenate(idx_rows + [jnp.zeros((pad, tm), I32)], axis=0)
    w_ref[...] = jnp.concatenate([tw, jnp.zeros((pad, tm), F32)], axis=0)


def norm2_router(x, g, mod, layer, router_wt, router_bias, grp, tm=ROW_BLK):
    t, d = x.shape
    n_exp = router_wt.shape[0]
    return pl.pallas_call(
        _norm2_kernel,
        grid=(t // tm,),
        in_specs=[
            pl.BlockSpec((tm, d), lambda i: (i, 0)),
            pl.BlockSpec((1, d), lambda i: (0, 0)),
            _mod_spec(grp, layer, d, 3, tm), _mod_spec(grp, layer, d, 4, tm),
            pl.BlockSpec((n_exp, d), lambda i: (0, 0)),
            pl.BlockSpec((n_exp, 1), lambda i: (0, 0)),
        ],
        out_specs=[pl.BlockSpec((tm, d // 2), lambda i: (i, 0)), pl.BlockSpec((tm, d), lambda i: (i, 0)),
                   pl.BlockSpec((8, tm), lambda i: (0, i)), pl.BlockSpec((8, tm), lambda i: (0, i))],
        out_shape=[jax.ShapeDtypeStruct((t, d // 2), jnp.uint32), jax.ShapeDtypeStruct((t, d), BF16),
                   jax.ShapeDtypeStruct((8, t), I32), jax.ShapeDtypeStruct((8, t), F32)],
        compiler_params=_cparams("arbitrary"),
        name="norm2_router",
    )(x, g.reshape(1, d), mod, mod, router_wt, router_bias.reshape(n_exp, 1))


def route_metadata(top_idx, n_exp, tm):
    k, t = top_idx.shape
    n_pairs = k * t
    n_tiles = n_pairs // tm + n_exp
    n_rows = n_tiles * tm
    e_flat = top_idx.reshape(-1)
    onehot = (e_flat[:, None] == jnp.arange(n_exp, dtype=I32)[None, :]).astype(I32)
    csum = jnp.cumsum(onehot, axis=0)
    counts = csum[-1]
    rank = jnp.take_along_axis(csum, e_flat[:, None], axis=1)[:, 0] - 1
    padded = (counts + tm - 1) // tm * tm
    ends = jnp.cumsum(padded)
    off = ends - padded
    pos = off[e_flat] + rank
    pair = jnp.arange(n_pairs, dtype=I32)
    rows = jnp.arange(n_rows, dtype=I32)
    spare = n_pairs + (rows // tm) % 2 * 8 + rows % 8
    row_dst = spare.at[pos].set(pair)
    row_token = jnp.where(row_dst < n_pairs, row_dst % t, 0)
    n_valid = ends[-1] // tm
    start = jnp.arange(n_tiles, dtype=I32) * tm
    te = jnp.minimum(jnp.searchsorted(ends, start, side="right").astype(I32), n_exp - 1)
    last_e = te[jnp.maximum(n_valid - 1, 0)]
    valid = jnp.arange(n_tiles, dtype=I32) < n_valid
    te = jnp.where(valid, te, last_e)
    rows_valid = jnp.where(valid, jnp.clip(counts[te] - (start - off[te]), 0, tm), 0).astype(I32)
    rows_valid = (rows_valid + 7) // 8 * 8
    return dict(tile_expert=te, n_valid=n_valid.reshape(1).astype(I32), rows_valid=rows_valid,
                row_token=row_token.reshape(n_tiles, 1, tm), row_dst=row_dst.reshape(n_tiles, 1, tm),
                n_tiles=n_tiles)


def _moe1_kernel(te_ref, nv_ref, tok_ref, tok_next_ref, h_hbm, wg_ref, wu_ref, o_ref, xbuf, sem, *, tm):
    del te_ref
    i = pl.program_id(0)
    n_valid = nv_ref[0]
    slot = i % 2

    def row_copy(tok, s, r):
        return pltpu.make_async_copy(h_hbm.at[pl.ds(tok, 1), :], xbuf.at[s, pl.ds(r, 1), :], sem.at[s])

    def gather(tref, s):
        def body(r, carry):
            row_copy(tref[0, r], s, r).start()
            return carry
        lax.fori_loop(0, tm, body, 0, unroll=8)

    @pl.when((i == 0) & (n_valid > 0))
    def _():
        gather(tok_ref, 0)

    def compute(prefetch_next):
        pltpu.make_async_copy(h_hbm.at[pl.ds(0, tm), :], xbuf.at[slot], sem.at[slot]).wait()
        x_lo, x_hi = _unpack_halves(xbuf[slot])
        half = x_lo.shape[1]
        if prefetch_next:
            for r in range(tm):
                row_copy(tok_next_ref[0, r], 1 - slot, r).start()

        def proj(w_ref):
            return (_dot(x_lo, w_ref[:half, :].astype(BF16)) + _dot(x_hi, w_ref[half:, :].astype(BF16)))

        o_ref[...] = (_silu(proj(wg_ref)) * proj(wu_ref)).astype(BF16)

    @pl.when(i + 1 < n_valid)
    def _():
        compute(True)

    @pl.when(i + 1 == n_valid)
    def _():
        compute(False)

    @pl.when(i >= n_valid)
    def _():
        o_ref[...] = jnp.zeros(o_ref.shape, BF16)


def moe_stage1(h2p, w_gate, w_up, layer, meta, tm=MOE_TM):
    d, de = w_gate.shape[-2:]
    n_tiles = meta["n_tiles"]
    last = n_tiles - 1
    wspec = pl.BlockSpec((None, None, d, de), lambda i, te, nv: (layer, te[i], 0, 0))
    grid_spec = pltpu.PrefetchScalarGridSpec(
        num_scalar_prefetch=2,
        grid=(n_tiles,),
        in_specs=[
            pl.BlockSpec((None, 1, tm), lambda i, te, nv: (i, 0, 0), memory_space=pltpu.SMEM),
            pl.BlockSpec((None, 1, tm), lambda i, te, nv: (jnp.minimum(i + 1, last), 0, 0),
                         memory_space=pltpu.SMEM),
            pl.BlockSpec(memory_space=pl.ANY),
            wspec, wspec,
        ],
        out_specs=pl.BlockSpec((tm, de), lambda i, te, nv: (i, 0)),
        scratch_shapes=[pltpu.VMEM((2, tm, d // 2), jnp.uint32), pltpu.SemaphoreType.DMA((2,))],
    )
    return pl.pallas_call(
        functools.partial(_moe1_kernel, tm=tm),
        grid_spec=grid_spec,
        out_shape=jax.ShapeDtypeStruct((n_tiles * tm, de), BF16),
        compiler_params=_cparams("arbitrary"),
        name="moe_stage1",
    )(meta["tile_expert"], meta["n_valid"], meta["row_token"], meta["row_token"], h2p, w_gate, w_up)


def _moe2_kernel(te_ref, nv_ref, nr_ref, dst_ref, hm_ref, wd_ref, out_hbm, ybuf, sem, *, tm, n_tiles):
    del te_ref
    i = pl.program_id(0)
    n_valid = nv_ref[0]
    slot = i % 2

    def wait_tile(j, s):
        n = pl.multiple_of(nr_ref[j], 8)
        pltpu.make_async_copy(ybuf.at[s, pl.ds(0, n), :], out_hbm.at[pl.ds(0, n), :], sem.at[s]).wait()

    @pl.when(i == 0)
    def _():
        spare = out_hbm.shape[0] - 16
        ybuf[1, 0:16, :] = jnp.zeros((16, ybuf.shape[2]), F32)
        init = pltpu.make_async_copy(ybuf.at[1, pl.ds(0, 16), :], out_hbm.at[pl.ds(spare, 16), :], sem.at[1])
        init.start()
        init.wait()

    @pl.when((i >= 2) & (i - 2 < n_valid))
    def _():
        wait_tile(i - 2, slot)

    @pl.when(i < n_valid)
    def _():
        ybuf[slot] = _dot(hm_ref[...], wd_ref[...].astype(BF16))

        def body(r8, carry):
            for k in range(8):
                r = r8 * 8 + k
                pltpu.make_async_copy(ybuf.at[slot, pl.ds(r, 1), :], out_hbm.at[pl.ds(dst_ref[0, r], 1), :],
                                      sem.at[slot]).start()
            return carry
        lax.fori_loop(0, nr_ref[i] // 8, body, 0)

    @pl.when(i == n_tiles - 1)
    def _():
        @pl.when((i >= 1) & (i - 1 < n_valid))
        def _():
            wait_tile(i - 1, 1 - slot)

        @pl.when(i < n_valid)
        def _():
            wait_tile(i, slot)


def moe_stage2(hmid, w_down, layer, meta, n_out_rows, tm=MOE_TM):
    de = hmid.shape[1]
    d = w_down.shape[-1]
    n_tiles = meta["n_tiles"]
    grid_spec = pltpu.PrefetchScalarGridSpec(
        num_scalar_prefetch=3,
        grid=(n_tiles,),
        in_specs=[
            pl.BlockSpec((None, 1, tm), lambda i, te, nv, nr: (i, 0, 0), memory_space=pltpu.SMEM),
            pl.BlockSpec((tm, de), lambda i, te, nv, nr: (i, 0)),
            pl.BlockSpec((None, None, de, d), lambda i, te, nv, nr: (layer, te[i], 0, 0)),
        ],
        out_specs=pl.BlockSpec(memory_space=pl.ANY),
        scratch_shapes=[pltpu.VMEM((2, tm, d), F32), pltpu.SemaphoreType.DMA((2,))],
    )
    return pl.pallas_call(
        functools.partial(_moe2_kernel, tm=tm, n_tiles=n_tiles),
        grid_spec=grid_spec,
        out_shape=jax.ShapeDtypeStruct((n_out_rows, d), F32),
        compiler_params=_cparams("arbitrary"),
        name="moe_stage2",
    )(meta["tile_expert"], meta["n_valid"], meta["rows_valid"], meta["row_dst"], hmid, w_down)


def _swiglu_in_kernel(x_ref, wg_ref, wu_ref, o_ref):
    x = x_ref[...]
    o_ref[...] = (_silu(_dot(x, wg_ref[...])) * _dot(x, wu_ref[...])).astype(BF16)


def shared_in(hb, wg, wu, tm=1024, tn=512):
    t, d = hb.shape
    n = wg.shape[1]
    tm, tn = _pick(t, tm), _pick(n, tn)
    wspec = pl.BlockSpec((d, tn), lambda i, j: (0, j))
    return pl.pallas_call(
        _swiglu_in_kernel,
        grid=(t // tm, n // tn),
        in_specs=[pl.BlockSpec((tm, d), lambda i, j: (i, 0)), wspec, wspec],
        out_specs=pl.BlockSpec((tm, tn), lambda i, j: (i, j)),
        out_shape=jax.ShapeDtypeStruct((t, n), BF16),
        compiler_params=_cparams("arbitrary", "arbitrary"),
        name="shared_in",
    )(hb, wg, wu)


def _moe_out_kernel(hs_ref, w_ref, x_ref, g_ref, tw_ref, *rest):
    slot_refs, o_ref = rest[:-1], rest[-1]
    tw = tw_ref[...]
    routed = slot_refs[0][...] * tw[:, 0:1]
    for s, r in enumerate(slot_refs[1:], start=1):
        routed = routed + r[...] * tw[:, s:s + 1]
    o_ref[...] = x_ref[...] + g_ref[...] * (routed + _dot(hs_ref[...], w_ref[...]))


def moe_out_residual(hs, w, x, mod, layer, slots, top_w_t, grp, tm=512, tn=1024):
    t, k = hs.shape
    d = w.shape[1]
    tm, tn = grp.row_tile(tm), _pick(d, tn)
    n_slots = TOP_K

    def slot_spec(s):
        return pl.BlockSpec((tm, tn), lambda i, j: (s * (t // tm) + i, j))

    return pl.pallas_call(
        _moe_out_kernel,
        grid=(t // tm, d // tn),
        in_specs=[
            pl.BlockSpec((tm, k), lambda i, j: (i, 0)),
            pl.BlockSpec((k, tn), lambda i, j: (0, j)),
            pl.BlockSpec((tm, tn), lambda i, j: (i, j)),
            _mod_spec(grp, layer, d, 5, tm, ncols=tn, two_d=True),
            pl.BlockSpec((tm, top_w_t.shape[1]), lambda i, j: (i, 0)),
        ] + [slot_spec(s) for s in range(n_slots)],
        out_specs=pl.BlockSpec((tm, tn), lambda i, j: (i, j)),
        out_shape=jax.ShapeDtypeStruct((t, d), F32),
        compiler_params=_cparams("arbitrary", "arbitrary"),
        name="moe_out_residual",
    )(hs, w, x, mod, top_w_t, *([slots] * n_slots))


def _final_norm_kernel(x_ref, g_ref, op_ref, os_ref, *, n_p_tiles):
    x = x_ref[...]
    y = x * lax.rsqrt(jnp.mean(x * x, axis=-1, keepdims=True) + EPS) * g_ref[...]

    @pl.when(pl.program_id(0) < n_p_tiles)
    def _():
        op_ref[...] = y

    @pl.when(pl.program_id(0) >= n_p_tiles)
    def _():
        os_ref[...] = y


def final_norm(x, g, grp, tm=256):
    t, d = x.shape
    tm = grp.row_tile(tm)
    n_p_tiles = grp.t_p // tm
    return pl.pallas_call(
        functools.partial(_final_norm_kernel, n_p_tiles=n_p_tiles),
        grid=(t // tm,),
        in_specs=[pl.BlockSpec((tm, d), lambda i: (i, 0)), pl.BlockSpec((1, d), lambda i: (0, 0))],
        out_specs=list(_two_group_specs((tm, d), n_p_tiles)),
        out_shape=[jax.ShapeDtypeStruct((grp.t_p, d), F32), jax.ShapeDtypeStruct((t - grp.t_p, d), F32)],
        compiler_params=_cparams("arbitrary"),
        name="final_norm",
    )(x, g.reshape(1, d))


def _embed_kernel(xp_ref, xs_ref, e_ref, o_ref, *, n_p_tiles):
    @pl.when(pl.program_id(0) < n_p_tiles)
    def _():
        o_ref[...] = xp_ref[...]

    @pl.when(pl.program_id(0) >= n_p_tiles)
    def _():
        o_ref[...] = xs_ref[...] + e_ref[...]


def embed_tokens(xp, xs, emb, grp, tm=256):
    d = xp.shape[1]
    tm = grp.row_tile(tm)
    n_p_tiles = grp.t_p // tm
    per_seq = grp.l_s // tm
    return pl.pallas_call(
        functools.partial(_embed_kernel, n_p_tiles=n_p_tiles),
        grid=(grp.t // tm,),
        in_specs=[
            *_two_group_specs((tm, d), n_p_tiles),
            pl.BlockSpec((tm, d), lambda i: (jnp.maximum(i - n_p_tiles, 0) % per_seq, 0)),
        ],
        out_specs=pl.BlockSpec((tm, d), lambda i: (i, 0)),
        out_shape=jax.ShapeDtypeStruct((grp.t, d), F32),
        compiler_params=_cparams("arbitrary"),
        name="embed_tokens",
    )(xp, xs, emb)


def _grid_sincos(n_tokens, dim):
    rows = n_tokens // GRID_W
    quarter = dim // 4
    omega = 1.0 / (10000.0 ** (jnp.arange(quarter, dtype=F32) / quarter))
    r = jnp.broadcast_to(jnp.arange(rows, dtype=F32)[:, None, None] * omega, (rows, GRID_W, quarter))
    cl = jnp.broadcast_to(jnp.arange(GRID_W, dtype=F32)[None, :, None] * omega, (rows, GRID_W, quarter))
    emb = jnp.concatenate([jnp.sin(r), jnp.cos(r), jnp.sin(cl), jnp.cos(cl)], axis=-1)
    return emb.reshape(n_tokens, dim)


def kernel(x_prompt, x_sample, state_delta, state_lru, c, c_ctx, w_ada, b_ada, norm1_g, w_in, dn_conv_w,
           dn_a_log, dn_dt_bias, dn_norm_g, w_branch_a, lru_conv_w, lru_conv_b, lru_wa, lru_ba, lru_wx,
           lru_bx, lru_lambda, w_branch_b, w_out, norm2_g, router_w, router_bias, w_exp_gate, w_exp_up,
           w_exp_down, w_sh_gate, w_sh_up, w_sh_down, final_norm_g):
    n_p, l_p, d = x_prompt.shape
    n_s, l_s, _ = x_sample.shape
    depth = w_ada.shape[0]
    heads, dk = state_delta.shape[3], state_delta.shape[4]
    qk_w = heads * dk
    v_w = heads * state_delta.shape[5]
    lru_w = state_lru.shape[-1]
    n_exp = router_w.shape[-1]
    grp = Groups(n_p, l_p, n_s, l_s)
    t = grp.t
    assert l_p % ROW_BLK == 0 and l_s % ROW_BLK == 0 and dk == LANES and v_w == qk_w

    x = embed_tokens(x_prompt.reshape(n_p * l_p, d), x_sample.reshape(n_s * l_s, d), _grid_sincos(l_s, d), grp)
    cond = jnp.concatenate([c_ctx[None], c, jnp.zeros((MOD_ROWS - 1 - n_s, d), F32)], axis=0)
    mod = ada_mod(cond.T, 1 + n_s, w_ada, b_ada).reshape(depth * MOD_ROWS, 1, 6 * d)

    g0 = 2 * qk_w + 2 * v_w
    g1 = g0 + 4 * heads
    col_q, col_k, col_v, col_z = 0, qk_w, 2 * qk_w, 2 * qk_w + v_w
    col_lx, col_ly = g0, g0 + lru_w
    col_ga, col_gb = g0 + 2 * lru_w, g0 + 2 * lru_w + d

    zero_delta = jnp.zeros((n_p, 2, heads, dk, LANES), F32)
    zero_lru = jnp.zeros((n_p, 2, lru_w), F32)
    new_delta, new_lru = [], []
    for l in range(depth):
        w_main = jnp.concatenate([w_in[l][:, :g0], w_in[l][:, g1:]], axis=1).astype(BF16)
        w_gates = jnp.pad(w_in[l][:, g0:g1], ((0, 0), (0, LANES - 4 * heads)))

        h, gates = norm1_gates(x, norm1_g[l], mod, l, w_gates, grp)
        proj = matmul(h, w_main, name="in_proj")
        gcol, gt = dn_gate_prep(gates, dn_a_log[l], dn_dt_bias[l], heads)

        dn_kw = dict(heads=heads, dk=dk, cols=(col_q, col_k, col_v, col_z))
        dn_p, s_delta = deltanet(proj, gcol, gt, dn_conv_w[l], dn_norm_g[l], zero_delta,
                                 n_seq=n_p, l=l_p, row0=0, **dn_kw)
        dn_s, _ = deltanet(proj, gcol, gt, dn_conv_w[l], dn_norm_g[l], state_delta[:, l],
                           n_seq=n_s, l=l_s, row0=grp.t_p, **dn_kw)
        lru_args = (proj, lru_conv_w[l], lru_conv_b[l], lru_wa[l], lru_wx[l], lru_ba[l], lru_bx[l],
                    lru_lambda[l])
        lru_p, s_lru = rglru(*lru_args, zero_lru, n_seq=n_p, l=l_p, row0=0, cols=(col_lx, col_ly))
        lru_s, _ = rglru(*lru_args, state_lru[:, l], n_seq=n_s, l=l_s, row0=grp.t_p, cols=(col_lx, col_ly))
        new_delta.append(s_delta)
        new_lru.append(s_lru)

        merged = branch_merge(dn_p, dn_s, lru_p, lru_s, w_branch_a[l].astype(BF16), w_branch_b[l].astype(BF16),
                              proj, (col_ga, col_gb), grp)
        x = out_proj_residual(merged, w_out[l].astype(BF16), x, mod, l, grp)

        h2, h2b, top_idx, top_w = norm2_router(x, norm2_g[l], mod, l, router_w[l].T, router_bias[l], grp)
        meta = route_metadata(top_idx[:TOP_K], n_exp, MOE_TM)
        hmid = moe_stage1(h2, w_exp_gate, w_exp_up, l, meta)
        slots = moe_stage2(hmid, w_exp_down, l, meta, TOP_K * t + 16)
        hs = shared_in(h2b, w_sh_gate[l].astype(BF16), w_sh_up[l].astype(BF16))
        x = moe_out_residual(hs, w_sh_down[l].astype(BF16), x, mod, l, slots, top_w.T, grp)

    y_prompt, y_sample = final_norm(x, final_norm_g, grp)
    return (y_prompt.reshape(n_p, l_p, d), y_sample.reshape(n_s, l_s, d),
            jnp.stack(new_delta, axis=1), jnp.stack(new_lru, axis=1))
```

```python
import functools
import math

import jax
import jax.numpy as jnp
from jax import lax
from jax.experimental import pallas as pl
from jax.experimental.pallas import tpu as pltpu

F32 = jnp.float32
BF16 = jnp.bfloat16
I32 = jnp.int32

EPS = 1e-6
GRID_W = 64
DN_CHUNK = 64
LRU_C = 8.0
TOP_K = 6
N_GROUPS = 8
TOPK_GROUPS = 4
ROUTED_SCALE = 2.5

V7X_VMEM_LIMIT_BYTES = 56 * 1024 * 1024
LANES = 128
ROW_BLK = 256
MOE_TM = 256
MOD_ROWS = 8
LRU_GROUP = 8


def _cparams(*sem):
    return pltpu.CompilerParams(dimension_semantics=sem, vmem_limit_bytes=V7X_VMEM_LIMIT_BYTES)


def _dot(a, b):
    return jnp.dot(a, b, preferred_element_type=F32)


def _dot_nt(a, b):
    return lax.dot_general(a, b, (((1,), (1,)), ((), ())), preferred_element_type=F32)


def _dot_tn(a, b):
    return lax.dot_general(a, b, (((0,), (0,)), ((), ())), preferred_element_type=F32)


def _split(a):
    hi = a.astype(BF16)
    lo = (a - hi.astype(F32)).astype(BF16)
    return hi, lo


def _dot3(a, b, dot=_dot):
    ah, al = _split(a)
    bh, bl = _split(b)
    return dot(ah, bh) + (dot(ah, bl) + dot(al, bh))


def _sigmoid(x):
    return 1.0 / (1.0 + jnp.exp(-x))


def _silu(x):
    return x * _sigmoid(x)


def _softplus(x):
    return jnp.maximum(x, 0.0) + jnp.log1p(jnp.exp(-jnp.abs(x)))


def _gelu_tanh(x):
    return 0.5 * x * (1.0 + jnp.tanh(math.sqrt(2.0 / math.pi) * (x + 0.044715 * (x * x * x))))


def _l2n(x):
    return x * lax.rsqrt(jnp.sum(x * x, axis=-1, keepdims=True) + EPS)


def _rms_mod(x, g, scale, shift):
    ms = jnp.mean(x * x, axis=-1, keepdims=True)
    y = x * lax.rsqrt(ms + EPS) * g
    return y * (1.0 + scale) + shift


def _pick(n, pref):
    c = pref
    while n % c:
        c //= 2
    return c


class Groups:
    def __init__(self, n_p, l_p, n_s, l_s):
        self.n_p, self.l_p, self.n_s, self.l_s = n_p, l_p, n_s, l_s
        self.t_p = n_p * l_p
        self.t = self.t_p + n_s * l_s

    def row_tile(self, pref):
        return _pick(math.gcd(self.t_p, self.l_s), pref)

    def mod_row(self, i, tm):
        start = i * tm
        return jnp.where(start < self.t_p, 0, 1 + (start - self.t_p) // self.l_s)


def _mod_spec(grp, layer, d, k, tm, ncols=None, two_d=False):
    if two_d:
        per = d // ncols
        return pl.BlockSpec((None, 1, ncols),
                            lambda i, j: (layer * MOD_ROWS + grp.mod_row(i, tm), 0, k * per + j))
    return pl.BlockSpec((None, 1, d), lambda i: (layer * MOD_ROWS + grp.mod_row(i, tm), 0, k))


ADA_K_SLABS = 4


def _ada_kernel(ct_ref, *rest, n_rows):
    w_refs, b_ref, o_ref = rest[:ADA_K_SLABS], rest[ADA_K_SLABS], rest[ADA_K_SLABS + 1]
    s = _silu(ct_ref[...])
    kq = w_refs[0].shape[0]
    rows = []
    for r in range(n_rows):
        acc = None
        for q, w_ref in enumerate(w_refs):
            part = jnp.sum(w_ref[...] * s[q * kq:(q + 1) * kq, r:r + 1], axis=0, keepdims=True)
            acc = part if acc is None else acc + part
        rows.append(acc)
    rows.append(jnp.zeros((MOD_ROWS - n_rows, o_ref.shape[1]), F32))
    o_ref[...] = jnp.concatenate(rows, axis=0) + b_ref[...]


def ada_mod(cond_t, n_rows, w_ada, b_ada, tn=1024):
    depth, d, n = w_ada.shape
    tn = _pick(n, tn)
    kq = d // ADA_K_SLABS

    def slab(q):
        return pl.BlockSpec((None, kq, tn), lambda l, j: (l, q, j))

    return pl.pallas_call(
        functools.partial(_ada_kernel, n_rows=n_rows),
        grid=(depth, n // tn),
        in_specs=[pl.BlockSpec((d, MOD_ROWS), lambda l, j: (0, 0))]
        + [slab(q) for q in range(ADA_K_SLABS)]
        + [pl.BlockSpec((None, 1, tn), lambda l, j: (l, 0, j))],
        out_specs=pl.BlockSpec((None, MOD_ROWS, tn), lambda l, j: (l, 0, j)),
        out_shape=jax.ShapeDtypeStruct((depth, MOD_ROWS, n), F32),
        compiler_params=_cparams("arbitrary", "arbitrary"),
        name="ada_mod",
    )(cond_t, *([w_ada] * ADA_K_SLABS), b_ada.reshape(depth, 1, n))


def _norm1_kernel(x_ref, g_ref, sh_ref, sc_ref, wg_ref, h_ref, gates_ref):
    h = _rms_mod(x_ref[...], g_ref[...], sc_ref[...], sh_ref[...])
    h_ref[...] = h.astype(BF16)
    gates_ref[...] = _dot3(h, wg_ref[...])


def _win_prep_kernel(w_ref, main_ref, gates_ref, *, g0, g1):
    w = w_ref[...]
    main_ref[:, :g0] = w[:, :g0].astype(BF16)
    main_ref[:, g0:] = w[:, g1:].astype(BF16)
    pad = jnp.zeros((w.shape[0], gates_ref.shape[1] - (g1 - g0)), F32)
    gates_ref[...] = jnp.concatenate([w[:, g0:g1], pad], axis=1)


def split_w_in(w_in, g0, g1, tk=128):
    depth, d, n = w_in.shape
    tk = _pick(d, tk)
    return pl.pallas_call(
        functools.partial(_win_prep_kernel, g0=g0, g1=g1),
        grid=(depth, d // tk),
        in_specs=[pl.BlockSpec((None, tk, n), lambda l, i: (l, i, 0))],
        out_specs=[pl.BlockSpec((None, tk, n - (g1 - g0)), lambda l, i: (l, i, 0)),
                   pl.BlockSpec((None, tk, LANES), lambda l, i: (l, i, 0))],
        out_shape=[jax.ShapeDtypeStruct((depth, d, n - (g1 - g0)), BF16),
                   jax.ShapeDtypeStruct((depth, d, LANES), F32)],
        compiler_params=_cparams("arbitrary", "arbitrary"),
        name="split_w_in",
    )(w_in)


def norm1_gates(x, g, mod, layer, w_gates, grp, tm=ROW_BLK):
    t, d = x.shape
    ng = w_gates.shape[-1]
    return pl.pallas_call(
        _norm1_kernel,
        grid=(t // tm,),
        in_specs=[
            pl.BlockSpec((tm, d), lambda i: (i, 0)),
            pl.BlockSpec((1, d), lambda i: (0, 0)),
            _mod_spec(grp, layer, d, 0, tm), _mod_spec(grp, layer, d, 1, tm),
            pl.BlockSpec((None, d, ng), lambda i: (layer, 0, 0)),
        ],
        out_specs=[pl.BlockSpec((tm, d), lambda i: (i, 0)), pl.BlockSpec((tm, ng), lambda i: (i, 0))],
        out_shape=[jax.ShapeDtypeStruct((t, d), BF16), jax.ShapeDtypeStruct((t, ng), F32)],
        compiler_params=_cparams("arbitrary"),
        name="norm1_gates",
    )(x, g.reshape(1, d), mod, mod, w_gates)


def _mm_kernel(x_ref, w_ref, o_ref):
    o_ref[...] = _dot(x_ref[...], w_ref[...])


def matmul(x, w, layer, tm=1024, tn=1024, name="matmul"):
    m, k = x.shape
    n = w.shape[-1]
    tm, tn = _pick(m, tm), _pick(n, tn)
    return pl.pallas_call(
        _mm_kernel,
        grid=(m // tm, n // tn),
        in_specs=[pl.BlockSpec((tm, k), lambda i, j: (i, 0)),
                  pl.BlockSpec((None, k, tn), lambda i, j: (layer, 0, j))],
        out_specs=pl.BlockSpec((tm, tn), lambda i, j: (i, j)),
        out_shape=jax.ShapeDtypeStruct((m, n), F32),
        compiler_params=_cparams("arbitrary", "arbitrary"),
        name=name,
    )(x, w)


def _dwconv(x, w):
    l = x.shape[0]
    row = lax.broadcasted_iota(I32, x.shape, 0)
    acc = x * w[2:3, :]
    for j, s in ((0, -2), (1, -1), (3, 1)):
        shifted = pltpu.roll(x, shift=(-s) % l, axis=0)
        valid = (row + s >= 0) & (row + s < l)
        acc = acc + jnp.where(valid, shifted, 0.0) * w[j:j + 1, :]
    return acc


_G_BETA, _G_EG, _G_EKD, _G_EGL, _G_GC = 0, 1, 2, 3, 4


def _unit_tri_inverses(mats):
    n = mats[0][0].shape[0]
    ri = lax.broadcasted_iota(I32, (n, n), 0)
    ci = lax.broadcasted_iota(I32, (n, n), 1)
    ts = [None] * len(mats)
    s = 1
    while s < DN_CHUNK:
        same = (ri // (2 * s)) == (ci // (2 * s))
        for m, (a, lower) in enumerate(mats):
            first, second = (ci, ri) if lower else (ri, ci)
            pair = same & ((first // s) % 2 == 0) & ((second // s) % 2 == 1)
            a_off = jnp.where(pair, a, 0.0)
            if s == 1:
                ts[m] = (ri == ci).astype(F32) - a_off
            else:
                tb = ts[m].astype(BF16)
                ts[m] = ts[m] - _dot(_dot(tb, a_off.astype(BF16)).astype(BF16), tb)
        s *= 2
    return ts


def _dn_kernel(q_ref, k_ref, v_ref, z_ref, gc_ref, gt_ref, cq_ref, ck_ref, cv_ref, ng_ref, s0_ref,
               o_ref, sfin_ref, qs, ks, vs, os_f, os_b, *, nb, dk, heads, hp):
    h0 = pl.program_id(1) * hp
    blk = ROW_BLK
    c_per = blk // DN_CHUNK

    qc = _silu(_dwconv(q_ref[...], cq_ref[...]))
    kc = _silu(_dwconv(k_ref[...], ck_ref[...]))
    vs[...] = _silu(_dwconv(v_ref[...], cv_ref[...]))
    for hh in range(hp):
        ls = slice(hh * LANES, (hh + 1) * LANES)
        qs[:, ls] = _l2n(qc[:, ls]) * (dk ** -0.5)
        ks[:, ls] = _l2n(kc[:, ls])

    ri = lax.broadcasted_iota(I32, (blk, blk), 0)
    ci = lax.broadcasted_iota(I32, (blk, blk), 1)
    same_chunk = (ri // DN_CHUNK) == (ci // DN_CHUNK)

    combos = [(hh, d) for hh in range(hp) for d in range(2)]

    def all_heads(jf, jb, states):
        pre = []
        for hh, d in combos:
            r0 = (jf if d == 0 else jb) * blk
            if not isinstance(r0, int):
                r0 = pl.multiple_of(r0, blk)
            rows = pl.ds(r0, blk)
            ls = slice(hh * LANES, (hh + 1) * LANES)
            q, k, v = qs[rows, ls], ks[rows, ls], vs[rows, ls]
            g = gc_ref[rows, ls]
            base = 5 * d
            beta = g[:, base + _G_BETA:base + _G_BETA + 1]
            eg = g[:, base + _G_EG:base + _G_EG + 1]
            ekd = g[:, base + _G_EKD:base + _G_EKD + 1]
            egl = g[:, base + _G_EGL:base + _G_EGL + 1]
            gcc = g[:, base + _G_GC:base + _G_GC + 1]
            gcr = gt_ref[jf if d == 0 else jb, pl.ds(d * heads + h0 + hh, 1), :]
            incl = same_chunk & ((ri >= ci) if d == 0 else (ri <= ci))
            strict = same_chunk & ((ri > ci) if d == 0 else (ri < ci))
            decay = jnp.where(incl, jnp.exp(jnp.where(incl, gcc - gcr, 0.0)), 0.0)
            kbf = k.astype(BF16)
            kb = k * beta
            a = jnp.where(strict, _dot_nt(kb.astype(BF16), kbf) * decay, 0.0)
            qk = _dot_nt(q.astype(BF16), kbf) * decay
            pre.append(dict(rows=rows, ls=ls, a=a, qk=qk, vb=(v * beta).astype(BF16),
                            kbe=(kb * eg).astype(BF16), q_dec=(q * eg).astype(BF16),
                            k_dec=(k * ekd).astype(BF16), egl=egl))
        ts = _unit_tri_inverses([(p["a"], d == 0) for p, (_, d) in zip(pre, combos)])
        for p, t in zip(pre, ts):
            tb = t.astype(BF16)
            p["u"] = _dot(tb, p["vb"])
            p["w"] = _dot(tb, p["kbe"]).astype(BF16)
        states = list(states)
        outs = [[None] * c_per for _ in combos]
        for step in range(c_per):
            for m, (p, (_, d)) in enumerate(zip(pre, combos)):
                c = step if d == 0 else c_per - 1 - step
                cs = slice(c * DN_CHUNK, (c + 1) * DN_CHUNK)
                sb = states[m].astype(BF16)
                v_new = p["u"][cs, :] - _dot(p["w"][cs, :], sb)
                vnb = v_new.astype(BF16)
                outs[m][c] = _dot(p["q_dec"][cs, :], sb) + _dot(p["qk"][cs, cs].astype(BF16), vnb)
                states[m] = (states[m] * p["egl"][c * DN_CHUNK:c * DN_CHUNK + 1, :]
                             + _dot_tn(p["k_dec"][cs, :], vnb))
        for m, (p, (_, d)) in enumerate(zip(pre, combos)):
            dst = os_f if d == 0 else os_b
            dst[p["rows"], p["ls"]] = jnp.concatenate(outs[m], axis=0)
        return tuple(states)

    states = tuple(s0_ref[d, hh] for hh in range(hp) for d in range(2))
    if nb == 1:
        states = all_heads(0, 0, states)
    else:
        states = lax.fori_loop(0, nb, lambda j, st: all_heads(j, nb - 1 - j, st), states)
    for hh in range(hp):
        for d in range(2):
            sfin_ref[d, hh] = states[2 * hh + d]

    o = os_f[...] + os_b[...]
    z = z_ref[...]
    for hh in range(hp):
        ls = slice(hh * LANES, (hh + 1) * LANES)
        oh = o[:, ls]
        oh = oh * lax.rsqrt(jnp.mean(oh * oh, axis=-1, keepdims=True) + EPS) * ng_ref[...]
        o_ref[:, ls] = (oh * _silu(z[:, ls])).astype(BF16)


def deltanet(proj, gcol, gt, conv_w, norm_g, s0, *, n_seq, l, row0, heads, dk, cols, hp=2):
    nb = l // ROW_BLK
    rb0 = row0 // l
    wl = hp * LANES
    cq, ck, cv, cz = (c // wl for c in cols)
    nh = heads // hp

    def col(cb):
        return pl.BlockSpec((l, wl), lambda b, h: (rb0 + b, cb + h))

    def cw(off):
        return pl.BlockSpec((4, wl), lambda b, h: (0, off + h))

    state_spec = pl.BlockSpec((None, 2, hp, dk, LANES), lambda b, h: (b, 0, h, 0, 0))
    return pl.pallas_call(
        functools.partial(_dn_kernel, nb=nb, dk=dk, heads=heads, hp=hp),
        grid=(n_seq, nh),
        in_specs=[
            col(cq), col(ck), col(cv), col(cz),
            pl.BlockSpec((l, wl), lambda b, h: (rb0 + b, h)),
            pl.BlockSpec((nb, gt.shape[1], ROW_BLK), lambda b, h: (rb0 + b, 0, 0)),
            cw(0), cw(nh), cw(2 * nh),
            pl.BlockSpec((1, LANES), lambda b, h: (0, 0)),
            state_spec,
        ],
        out_specs=[pl.BlockSpec((l, wl), lambda b, h: (b, h)), state_spec],
        out_shape=[jax.ShapeDtypeStruct((n_seq * l, heads * LANES), BF16),
                   jax.ShapeDtypeStruct((n_seq, 2, heads, dk, LANES), F32)],
        scratch_shapes=[pltpu.VMEM((l, wl), F32)] * 5,
        compiler_params=_cparams("arbitrary", "arbitrary"),
        name=f"deltanet_l{l}",
    )(proj, proj, proj, proj, gcol, gt, conv_w, conv_w, conv_w, norm_g.reshape(1, LANES), s0)


def dn_gate_prep(gates, a_log, dt_bias, heads):
    t = gates.shape[0]
    n = t // DN_CHUNK
    beta = jax.nn.sigmoid(gates[:, :2 * heads]).reshape(n, DN_CHUNK, 2, heads)
    g = -jnp.exp(a_log)[None] * jax.nn.softplus(gates[:, 2 * heads:4 * heads].reshape(t, 2, heads) + dt_bias[None])
    g = g.reshape(n, DN_CHUNK, 2, heads)
    gc_f = jnp.cumsum(g[:, :, 0], axis=1)
    gc_b = jnp.flip(jnp.cumsum(jnp.flip(g[:, :, 1], axis=1), axis=1), axis=1)
    gc = jnp.stack([gc_f, gc_b], axis=2)
    glast = jnp.stack([gc_f[:, -1], gc_b[:, 0]], axis=1)[:, None]
    quant = jnp.stack([beta, jnp.exp(gc), jnp.exp(glast - gc),
                       jnp.broadcast_to(jnp.exp(glast), gc.shape), gc], axis=-1)
    quant = jnp.transpose(quant, (0, 1, 3, 2, 4)).reshape(t, heads, 10)
    gcol = jnp.pad(quant, ((0, 0), (0, 0), (0, LANES - 10))).reshape(t, heads * LANES)
    gt = jnp.transpose(gc.reshape(t // ROW_BLK, ROW_BLK, 2 * heads), (0, 2, 1))
    return gcol, gt


def _lru_kernel(x_ref, y_ref, cw_ref, cb_ref, wa_ref, wx_ref, ba_ref, bx_ref, lam_ref, h0_ref,
                o_ref, sfin_ref, a_s, b_s, hin_s, *, l, cw):
    xc = _dwconv(x_ref[...], cw_ref[...]) + cb_ref[...]
    for d in range(2):
        sp = _softplus(-lam_ref[d:d + 1, :])
        for s in range(cw // LANES):
            cs = slice(s * LANES, (s + 1) * LANES)
            xs = xc[:, cs]
            xb = xs.astype(BF16)
            r = _sigmoid(_dot(xb, wa_ref[d, s].astype(BF16)) + ba_ref[d:d + 1, cs])
            i = _sigmoid(_dot(xb, wx_ref[d, s].astype(BF16)) + bx_ref[d:d + 1, cs])
            log_a = -LRU_C * r * sp[:, cs]
            a = jnp.exp(log_a)
            b = jnp.sqrt(jnp.tanh(-log_a) * (a * a + 1.0)) * i * xs
            sub = lax.broadcasted_iota(I32, a.shape, 0) % LRU_GROUP
            for sh in (1, 2, 4):
                shift = sh if d == 0 else l - sh
                ok = (sub >= sh) if d == 0 else (sub < LRU_GROUP - sh)
                a_n = pltpu.roll(a, shift=shift, axis=0)
                b_n = pltpu.roll(b, shift=shift, axis=0)
                b = jnp.where(ok, a * b_n + b, b)
                a = jnp.where(ok, a * a_n, a)
            a_s[d, :, cs] = a
            b_s[d, :, cs] = b

    n_grp = l // LRU_GROUP

    def carry_groups(d):
        def body(i, h):
            g = i if d == 0 else n_grp - 1 - i
            r0 = pl.multiple_of(g * LRU_GROUP, LRU_GROUP)
            hin_s[d, pl.ds(r0, LRU_GROUP), :] = jnp.broadcast_to(h, (LRU_GROUP, cw))
            last = r0 + (LRU_GROUP - 1 if d == 0 else 0)
            return a_s[d, pl.ds(last, 1), :] * h + b_s[d, pl.ds(last, 1), :]
        return lax.fori_loop(0, n_grp, body, h0_ref[d:d + 1, :], unroll=4)

    sfin_ref[0:1, :] = carry_groups(0)
    sfin_ref[1:2, :] = carry_groups(1)
    h = (a_s[0] * hin_s[0] + b_s[0]) + (a_s[1] * hin_s[1] + b_s[1])
    o_ref[...] = (h * _gelu_tanh(y_ref[...])).astype(BF16)


def rglru(proj, conv_w, conv_b, wa, wx, ba, bx, lam, h0, *, n_seq, l, row0, cols, cw=512):
    width = conv_w.shape[1]
    cw = _pick(width, cw)
    rb0 = row0 // l
    cx, cy = (c // cw for c in cols)
    nsub = cw // LANES
    vec2 = pl.BlockSpec((2, cw), lambda b, n: (0, n))
    wblk = pl.BlockSpec((2, nsub, LANES, LANES), lambda b, n: (0, n, 0, 0))
    state_spec = pl.BlockSpec((None, 2, cw), lambda b, n: (b, 0, n))
    return pl.pallas_call(
        functools.partial(_lru_kernel, l=l, cw=cw),
        grid=(n_seq, width // cw),
        in_specs=[
            pl.BlockSpec((l, cw), lambda b, n: (rb0 + b, cx + n)),
            pl.BlockSpec((l, cw), lambda b, n: (rb0 + b, cy + n)),
            pl.BlockSpec((4, cw), lambda b, n: (0, n)),
            pl.BlockSpec((1, cw), lambda b, n: (0, n)),
            wblk, wblk, vec2, vec2, vec2, state_spec,
        ],
        out_specs=[pl.BlockSpec((l, cw), lambda b, n: (b, n)), state_spec],
        out_shape=[jax.ShapeDtypeStruct((n_seq * l, width), BF16),
                   jax.ShapeDtypeStruct((n_seq, 2, width), F32)],
        scratch_shapes=[pltpu.VMEM((2, l, cw), F32)] * 3,
        compiler_params=_cparams("arbitrary", "arbitrary"),
        name=f"rglru_l{l}",
    )(proj, proj, conv_w, conv_b.reshape(1, width), wa, wx, ba, bx, lam, h0)


def _branch_kernel(ap_ref, as_ref, bp_ref, bs_ref, wa_ref, wb_ref, ga_ref, gb_ref, o_ref, *, n_p_tiles):
    def emit(a_ref, b_ref):
        oa = _dot(a_ref[...], wa_ref[...])
        ob = _dot(b_ref[...], wb_ref[...])
        o_ref[...] = (_sigmoid(ga_ref[...]) * oa + _sigmoid(gb_ref[...]) * ob).astype(BF16)

    @pl.when(pl.program_id(0) < n_p_tiles)
    def _():
        emit(ap_ref, bp_ref)

    @pl.when(pl.program_id(0) >= n_p_tiles)
    def _():
        emit(as_ref, bs_ref)


def _two_group_specs(shape, n_p_tiles):
    return (pl.BlockSpec(shape, lambda i, *_: (jnp.minimum(i, n_p_tiles - 1), 0)),
            pl.BlockSpec(shape, lambda i, *_: (jnp.maximum(i - n_p_tiles, 0), 0)))


def branch_merge(dn_p, dn_s, lru_p, lru_s, wa, wb, proj, cols, grp, tm=1024, tn=512):
    ka, kb = dn_p.shape[1], lru_p.shape[1]
    t = grp.t
    d = wa.shape[1]
    tm, tn = grp.row_tile(tm), _pick(d, tn)
    n_p_tiles = grp.t_p // tm
    cga, cgb = (c // tn for c in cols)
    return pl.pallas_call(
        functools.partial(_branch_kernel, n_p_tiles=n_p_tiles),
        grid=(t // tm, d // tn),
        in_specs=[
            *_two_group_specs((tm, ka), n_p_tiles),
            *_two_group_specs((tm, kb), n_p_tiles),
            pl.BlockSpec((ka, tn), lambda i, j: (0, j)),
            pl.BlockSpec((kb, tn), lambda i, j: (0, j)),
            pl.BlockSpec((tm, tn), lambda i, j: (i, cga + j)),
            pl.BlockSpec((tm, tn), lambda i, j: (i, cgb + j)),
        ],
        out_specs=pl.BlockSpec((tm, tn), lambda i, j: (i, j)),
        out_shape=jax.ShapeDtypeStruct((t, d), BF16),
        compiler_params=_cparams("arbitrary", "arbitrary"),
        name="branch_merge",
    )(dn_p, dn_s, lru_p, lru_s, wa, wb, proj, proj)


def _resid_kernel(m_ref, w_ref, x_ref, g_ref, o_ref):
    o_ref[...] = x_ref[...] + g_ref[...] * _dot(m_ref[...], w_ref[...])


def out_proj_residual(m, w, x, mod, layer, grp, tm=1024, tn=512):
    t, k = m.shape
    d = w.shape[1]
    tm, tn = grp.row_tile(tm), _pick(d, tn)
    return pl.pallas_call(
        _resid_kernel,
        grid=(t // tm, d // tn),
        in_specs=[
            pl.BlockSpec((tm, k), lambda i, j: (i, 0)),
            pl.BlockSpec((k, tn), lambda i, j: (0, j)),
            pl.BlockSpec((tm, tn), lambda i, j: (i, j)),
            _mod_spec(grp, layer, d, 2, tm, ncols=tn, two_d=True),
        ],
        out_specs=pl.BlockSpec((tm, tn), lambda i, j: (i, j)),
        out_shape=jax.ShapeDtypeStruct((t, d), F32),
        compiler_params=_cparams("arbitrary", "arbitrary"),
        name="out_proj_residual",
    )(m, w, x, mod)


def _pack_halves(xb):
    n = xb.shape[1] // 2
    lo = pltpu.bitcast(xb[:, :n].astype(F32), jnp.uint32)
    hi = pltpu.bitcast(xb[:, n:].astype(F32), jnp.uint32)
    return (lo >> 16) | (hi & jnp.uint32(0xFFFF0000))


def _unpack_halves(xp):
    lo = pltpu.bitcast(xp << 16, F32).astype(BF16)
    hi = pltpu.bitcast(xp & jnp.uint32(0xFFFF0000), F32).astype(BF16)
    return lo, hi


def _first_argmax(x, idx, n, axis):
    m = jnp.max(x, axis=axis, keepdims=True)
    return jnp.min(jnp.where(x == m, idx, n), axis=axis, keepdims=True)


def _norm2_kernel(x_ref, g_ref, sh_ref, sc_ref, rw_ref, rb_ref, h_ref, hb_ref, idx_ref, w_ref):
    h = _rms_mod(x_ref[...], g_ref[...], sc_ref[...], sh_ref[...])
    hb = h.astype(BF16)
    hb_ref[...] = hb
    h_ref[...] = _pack_halves(hb)
    tm = h.shape[0]
    n_exp = rw_ref.shape[0]
    per = n_exp // N_GROUPS
    scores = _sigmoid(_dot3(rw_ref[...], h, dot=_dot_nt))
    sel = scores + rb_ref[...]
    neg = -jnp.inf
    grp = sel.reshape(N_GROUPS, per, tm)
    sub = lax.broadcasted_iota(I32, grp.shape, 1)
    m1 = jnp.max(grp, axis=1, keepdims=True)
    first = jnp.min(jnp.where(grp == m1, sub, per), axis=1, keepdims=True)
    m2 = jnp.max(jnp.where(sub == first, neg, grp), axis=1, keepdims=True)
    gscore = (m1 + m2).reshape(N_GROUPS, tm)
    gi = lax.broadcasted_iota(I32, gscore.shape, 0)
    gmask = jnp.zeros(gscore.shape, F32)
    for _ in range(TOPK_GROUPS):
        pick = gi == _first_argmax(gscore, gi, N_GROUPS, 0)
        gmask = jnp.where(pick, 1.0, gmask)
        gscore = jnp.where(pick, neg, gscore)
    emask = jnp.broadcast_to(gmask.reshape(N_GROUPS, 1, tm), grp.shape).reshape(n_exp, tm)
    cand = jnp.where(emask > 0.0, sel, neg)
    ei = lax.broadcasted_iota(I32, cand.shape, 0)
    idx_rows, w_rows = [], []
    for _ in range(TOP_K):
        f = _first_argmax(cand, ei, n_exp, 0)
        pick = ei == f
        idx_rows.append(f)
        w_rows.append(jnp.sum(jnp.where(pick, scores, 0.0), axis=0, keepdims=True))
        cand = jnp.where(pick, neg, cand)
    tw = jnp.concatenate(w_rows, axis=0)
    tw = tw / jnp.sum(tw, axis=0, keepdims=True) * ROUTED_SCALE
    pad = 8 - TOP_K
    idx_ref[...] = jnp.concatenate(idx_rows + [jnp.zeros((pad, tm), I32)], axis=0)
    w_ref[...] = jnp.concatenate([tw, jnp.zeros((pad, tm), F32)], axis=0)


def norm2_router(x, g, mod, layer, router_wt, router_bias, grp, tm=ROW_BLK):
    t, d = x.shape
    n_exp = router_wt.shape[0]
    return pl.pallas_call(
        _norm2_kernel,
        grid=(t // tm,),
        in_specs=[
            pl.BlockSpec((tm, d), lambda i: (i, 0)),
            pl.BlockSpec((1, d), lambda i: (0, 0)),
            _mod_spec(grp, layer, d, 3, tm), _mod_spec(grp, layer, d, 4, tm),
            pl.BlockSpec((n_exp, d), lambda i: (0, 0)),
            pl.BlockSpec((n_exp, 1), lambda i: (0, 0)),
        ],
        out_specs=[pl.BlockSpec((tm, d // 2), lambda i: (i, 0)), pl.BlockSpec((tm, d), lambda i: (i, 0)),
                   pl.BlockSpec((8, tm), lambda i: (0, i)), pl.BlockSpec((8, tm), lambda i: (0, i))],
        out_shape=[jax.ShapeDtypeStruct((t, d // 2), jnp.uint32), jax.ShapeDtypeStruct((t, d), BF16),
                   jax.ShapeDtypeStruct((8, t), I32), jax.ShapeDtypeStruct((8, t), F32)],
        compiler_params=_cparams("arbitrary"),
        name="norm2_router",
    )(x, g.reshape(1, d), mod, mod, router_wt, router_bias.reshape(n_exp, 1))


def route_metadata(top_idx, n_exp, tm):
    k, t = top_idx.shape
    n_pairs = k * t
    n_tiles = n_pairs // tm + n_exp
    n_rows = n_tiles * tm
    e_flat = top_idx.reshape(-1)
    onehot = (e_flat[:, None] == jnp.arange(n_exp, dtype=I32)[None, :]).astype(I32)
    csum = jnp.cumsum(onehot, axis=0)
    counts = csum[-1]
    rank = jnp.take_along_axis(csum, e_flat[:, None], axis=1)[:, 0] - 1
    padded = (counts + tm - 1) // tm * tm
    ends = jnp.cumsum(padded)
    off = ends - padded
    pos = off[e_flat] + rank
    pair = jnp.arange(n_pairs, dtype=I32)
    rows = jnp.arange(n_rows, dtype=I32)
    spare = n_pairs + (rows // tm) % 2 * 8 + rows % 8
    row_dst = spare.at[pos].set(pair)
    row_token = jnp.where(row_dst < n_pairs, row_dst % t, 0)
    n_valid = ends[-1] // tm
    start = jnp.arange(n_tiles, dtype=I32) * tm
    te = jnp.minimum(jnp.searchsorted(ends, start, side="right").astype(I32), n_exp - 1)
    last_e = te[jnp.maximum(n_valid - 1, 0)]
    valid = jnp.arange(n_tiles, dtype=I32) < n_valid
    te = jnp.where(valid, te, last_e)
    rows_valid = jnp.where(valid, jnp.clip(counts[te] - (start - off[te]), 0, tm), 0).astype(I32)
    rows_valid = (rows_valid + 7) // 8 * 8
    return dict(tile_expert=te, n_valid=n_valid.reshape(1).astype(I32), rows_valid=rows_valid,
                row_token=row_token.reshape(n_tiles, 1, tm), row_dst=row_dst.reshape(n_tiles, 1, tm),
                n_tiles=n_tiles)


def _moe1_kernel(te_ref, nv_ref, tok_ref, tok_next_ref, h_hbm, wg_ref, wu_ref, o_ref, xbuf, sem, *, tm):
    del te_ref
    i = pl.program_id(0)
    n_valid = nv_ref[0]
    slot = i % 2

    def row_copy(tok, s, r):
        return pltpu.make_async_copy(h_hbm.at[pl.ds(tok, 1), :], xbuf.at[s, pl.ds(r, 1), :], sem.at[s])

    def gather(tref, s):
        def body(r, carry):
            row_copy(tref[0, r], s, r).start()
            return carry
        lax.fori_loop(0, tm, body, 0, unroll=8)

    @pl.when((i == 0) & (n_valid > 0))
    def _():
        gather(tok_ref, 0)

    @pl.when(i + 1 < n_valid)
    def _():
        gather(tok_next_ref, 1 - slot)

    @pl.when(i < n_valid)
    def _():
        pltpu.make_async_copy(h_hbm.at[pl.ds(0, tm), :], xbuf.at[slot], sem.at[slot]).wait()
        x_lo, x_hi = _unpack_halves(xbuf[slot])
        half = x_lo.shape[1]

        def proj(w_ref):
            return (_dot(x_lo, w_ref[:half, :].astype(BF16)) + _dot(x_hi, w_ref[half:, :].astype(BF16)))

        o_ref[...] = (_silu(proj(wg_ref)) * proj(wu_ref)).astype(BF16)

    @pl.when(i >= n_valid)
    def _():
        o_ref[...] = jnp.zeros(o_ref.shape, BF16)


def moe_stage1(h2p, w_gate, w_up, layer, meta, tm=MOE_TM):
    d, de = w_gate.shape[-2:]
    n_tiles = meta["n_tiles"]
    last = n_tiles - 1
    wspec = pl.BlockSpec((None, None, d, de), lambda i, te, nv: (layer, te[i], 0, 0))
    grid_spec = pltpu.PrefetchScalarGridSpec(
        num_scalar_prefetch=2,
        grid=(n_tiles,),
        in_specs=[
            pl.BlockSpec((None, 1, tm), lambda i, te, nv: (i, 0, 0), memory_space=pltpu.SMEM),
            pl.BlockSpec((None, 1, tm), lambda i, te, nv: (jnp.minimum(i + 1, last), 0, 0),
                         memory_space=pltpu.SMEM),
            pl.BlockSpec(memory_space=pl.ANY),
            wspec, wspec,
        ],
        out_specs=pl.BlockSpec((tm, de), lambda i, te, nv: (i, 0)),
        scratch_shapes=[pltpu.VMEM((2, tm, d // 2), jnp.uint32), pltpu.SemaphoreType.DMA((2,))],
    )
    return pl.pallas_call(
        functools.partial(_moe1_kernel, tm=tm),
        grid_spec=grid_spec,
        out_shape=jax.ShapeDtypeStruct((n_tiles * tm, de), BF16),
        compiler_params=_cparams("arbitrary"),
        name="moe_stage1",
    )(meta["tile_expert"], meta["n_valid"], meta["row_token"], meta["row_token"], h2p, w_gate, w_up)


def _moe2_kernel(te_ref, nv_ref, nr_ref, dst_ref, hm_ref, wd_ref, out_hbm, ybuf, sem, *, tm, n_tiles):
    del te_ref
    i = pl.program_id(0)
    n_valid = nv_ref[0]
    slot = i % 2

    def wait_tile(j, s):
        n = pl.multiple_of(nr_ref[j], 8)
        pltpu.make_async_copy(ybuf.at[s, pl.ds(0, n), :], out_hbm.at[pl.ds(0, n), :], sem.at[s]).wait()

    @pl.when(i == 0)
    def _():
        spare = out_hbm.shape[0] - 16
        ybuf[1, 0:16, :] = jnp.zeros((16, ybuf.shape[2]), F32)
        init = pltpu.make_async_copy(ybuf.at[1, pl.ds(0, 16), :], out_hbm.at[pl.ds(spare, 16), :], sem.at[1])
        init.start()
        init.wait()

    @pl.when((i >= 2) & (i - 2 < n_valid))
    def _():
        wait_tile(i - 2, slot)

    @pl.when(i < n_valid)
    def _():
        ybuf[slot] = _dot(hm_ref[...], wd_ref[...].astype(BF16))

        def body(r8, carry):
            for k in range(8):
                r = r8 * 8 + k
                pltpu.make_async_copy(ybuf.at[slot, pl.ds(r, 1), :], out_hbm.at[pl.ds(dst_ref[0, r], 1), :],
                                      sem.at[slot]).start()
            return carry
        lax.fori_loop(0, nr_ref[i] // 8, body, 0)

    @pl.when(i == n_tiles - 1)
    def _():
        @pl.when((i >= 1) & (i - 1 < n_valid))
        def _():
            wait_tile(i - 1, 1 - slot)

        @pl.when(i < n_valid)
        def _():
            wait_tile(i, slot)


def moe_stage2(hmid, w_down, layer, meta, n_out_rows, tm=MOE_TM):
    de = hmid.shape[1]
    d = w_down.shape[-1]
    n_tiles = meta["n_tiles"]
    grid_spec = pltpu.PrefetchScalarGridSpec(
        num_scalar_prefetch=3,
        grid=(n_tiles,),
        in_specs=[
            pl.BlockSpec((None, 1, tm), lambda i, te, nv, nr: (i, 0, 0), memory_space=pltpu.SMEM),
            pl.BlockSpec((tm, de), lambda i, te, nv, nr: (i, 0)),
            pl.BlockSpec((None, None, de, d), lambda i, te, nv, nr: (layer, te[i], 0, 0)),
        ],
        out_specs=pl.BlockSpec(memory_space=pl.ANY),
        scratch_shapes=[pltpu.VMEM((2, tm, d), F32), pltpu.SemaphoreType.DMA((2,))],
    )
    return pl.pallas_call(
        functools.partial(_moe2_kernel, tm=tm, n_tiles=n_tiles),
        grid_spec=grid_spec,
        out_shape=jax.ShapeDtypeStruct((n_out_rows, d), F32),
        compiler_params=_cparams("arbitrary"),
        name="moe_stage2",
    )(meta["tile_expert"], meta["n_valid"], meta["rows_valid"], meta["row_dst"], hmid, w_down)


def _swiglu_in_kernel(x_ref, wg_ref, wu_ref, o_ref):
    x = x_ref[...]
    o_ref[...] = (_silu(_dot(x, wg_ref[...])) * _dot(x, wu_ref[...])).astype(BF16)


def shared_in(hb, wg, wu, tm=1024, tn=512):
    t, d = hb.shape
    n = wg.shape[1]
    tm, tn = _pick(t, tm), _pick(n, tn)
    wspec = pl.BlockSpec((d, tn), lambda i, j: (0, j))
    return pl.pallas_call(
        _swiglu_in_kernel,
        grid=(t // tm, n // tn),
        in_specs=[pl.BlockSpec((tm, d), lambda i, j: (i, 0)), wspec, wspec],
        out_specs=pl.BlockSpec((tm, tn), lambda i, j: (i, j)),
        out_shape=jax.ShapeDtypeStruct((t, n), BF16),
        compiler_params=_cparams("arbitrary", "arbitrary"),
        name="shared_in",
    )(hb, wg, wu)


def _moe_out_kernel(hs_ref, w_ref, x_ref, g_ref, tw_ref, *rest):
    slot_refs, o_ref = rest[:-1], rest[-1]
    tw = tw_ref[...]
    routed = slot_refs[0][...] * tw[:, 0:1]
    for s, r in enumerate(slot_refs[1:], start=1):
        routed = routed + r[...] * tw[:, s:s + 1]
    o_ref[...] = x_ref[...] + g_ref[...] * (routed + _dot(hs_ref[...], w_ref[...]))


def moe_out_residual(hs, w, x, mod, layer, slots, top_w_t, grp, tm=512, tn=1024):
    t, k = hs.shape
    d = w.shape[1]
    tm, tn = grp.row_tile(tm), _pick(d, tn)
    n_slots = TOP_K

    def slot_spec(s):
        return pl.BlockSpec((tm, tn), lambda i, j: (s * (t // tm) + i, j))

    return pl.pallas_call(
        _moe_out_kernel,
        grid=(t // tm, d // tn),
        in_specs=[
            pl.BlockSpec((tm, k), lambda i, j: (i, 0)),
            pl.BlockSpec((k, tn), lambda i, j: (0, j)),
            pl.BlockSpec((tm, tn), lambda i, j: (i, j)),
            _mod_spec(grp, layer, d, 5, tm, ncols=tn, two_d=True),
            pl.BlockSpec((tm, top_w_t.shape[1]), lambda i, j: (i, 0)),
        ] + [slot_spec(s) for s in range(n_slots)],
        out_specs=pl.BlockSpec((tm, tn), lambda i, j: (i, j)),
        out_shape=jax.ShapeDtypeStruct((t, d), F32),
        compiler_params=_cparams("arbitrary", "arbitrary"),
        name="moe_out_residual",
    )(hs, w, x, mod, top_w_t, *([slots] * n_slots))


def _final_norm_kernel(x_ref, g_ref, op_ref, os_ref, *, n_p_tiles):
    x = x_ref[...]
    y = x * lax.rsqrt(jnp.mean(x * x, axis=-1, keepdims=True) + EPS) * g_ref[...]

    @pl.when(pl.program_id(0) < n_p_tiles)
    def _():
        op_ref[...] = y

    @pl.when(pl.program_id(0) >= n_p_tiles)
    def _():
        os_ref[...] = y


def final_norm(x, g, grp, tm=256):
    t, d = x.shape
    tm = grp.row_tile(tm)
    n_p_tiles = grp.t_p // tm
    return pl.pallas_call(
        functools.partial(_final_norm_kernel, n_p_tiles=n_p_tiles),
        grid=(t // tm,),
        in_specs=[pl.BlockSpec((tm, d), lambda i: (i, 0)), pl.BlockSpec((1, d), lambda i: (0, 0))],
        out_specs=list(_two_group_specs((tm, d), n_p_tiles)),
        out_shape=[jax.ShapeDtypeStruct((grp.t_p, d), F32), jax.ShapeDtypeStruct((t - grp.t_p, d), F32)],
        compiler_params=_cparams("arbitrary"),
        name="final_norm",
    )(x, g.reshape(1, d))


def _embed_kernel(xp_ref, xs_ref, e_ref, o_ref, *, n_p_tiles):
    @pl.when(pl.program_id(0) < n_p_tiles)
    def _():
        o_ref[...] = xp_ref[...]

    @pl.when(pl.program_id(0) >= n_p_tiles)
    def _():
        o_ref[...] = xs_ref[...] + e_ref[...]


def embed_tokens(xp, xs, emb, grp, tm=256):
    d = xp.shape[1]
    tm = grp.row_tile(tm)
    n_p_tiles = grp.t_p // tm
    per_seq = grp.l_s // tm
    return pl.pallas_call(
        functools.partial(_embed_kernel, n_p_tiles=n_p_tiles),
        grid=(grp.t // tm,),
        in_specs=[
            *_two_group_specs((tm, d), n_p_tiles),
            pl.BlockSpec((tm, d), lambda i: (jnp.maximum(i - n_p_tiles, 0) % per_seq, 0)),
        ],
        out_specs=pl.BlockSpec((tm, d), lambda i: (i, 0)),
        out_shape=jax.ShapeDtypeStruct((grp.t, d), F32),
        compiler_params=_cparams("arbitrary"),
        name="embed_tokens",
    )(xp, xs, emb)


def _grid_sincos(n_tokens, dim):
    rows = n_tokens // GRID_W
    quarter = dim // 4
    omega = 1.0 / (10000.0 ** (jnp.arange(quarter, dtype=F32) / quarter))
    r = jnp.broadcast_to(jnp.arange(rows, dtype=F32)[:, None, None] * omega, (rows, GRID_W, quarter))
    cl = jnp.broadcast_to(jnp.arange(GRID_W, dtype=F32)[None, :, None] * omega, (rows, GRID_W, quarter))
    emb = jnp.concatenate([jnp.sin(r), jnp.cos(r), jnp.sin(cl), jnp.cos(cl)], axis=-1)
    return emb.reshape(n_tokens, dim)


def kernel(x_prompt, x_sample, state_delta, state_lru, c, c_ctx, w_ada, b_ada, norm1_g, w_in, dn_conv_w,
           dn_a_log, dn_dt_bias, dn_norm_g, w_branch_a, lru_conv_w, lru_conv_b, lru_wa, lru_ba, lru_wx,
           lru_bx, lru_lambda, w_branch_b, w_out, norm2_g, router_w, router_bias, w_exp_gate, w_exp_up,
           w_exp_down, w_sh_gate, w_sh_up, w_sh_down, final_norm_g):
    n_p, l_p, d = x_prompt.shape
    n_s, l_s, _ = x_sample.shape
    depth = w_ada.shape[0]
    heads, dk = state_delta.shape[3], state_delta.shape[4]
    qk_w = heads * dk
    v_w = heads * state_delta.shape[5]
    lru_w = state_lru.shape[-1]
    n_exp = router_w.shape[-1]
    grp = Groups(n_p, l_p, n_s, l_s)
    t = grp.t
    assert l_p % ROW_BLK == 0 and l_s % ROW_BLK == 0 and dk == LANES and v_w == qk_w

    x = embed_tokens(x_prompt.reshape(n_p * l_p, d), x_sample.reshape(n_s * l_s, d), _grid_sincos(l_s, d), grp)
    cond = jnp.concatenate([c_ctx[None], c, jnp.zeros((MOD_ROWS - 1 - n_s, d), F32)], axis=0)
    mod = ada_mod(cond.T, 1 + n_s, w_ada, b_ada).reshape(depth * MOD_ROWS, 1, 6 * d)

    g0 = 2 * qk_w + 2 * v_w
    g1 = g0 + 4 * heads
    col_q, col_k, col_v, col_z = 0, qk_w, 2 * qk_w, 2 * qk_w + v_w
    col_lx, col_ly = g0, g0 + lru_w
    col_ga, col_gb = g0 + 2 * lru_w, g0 + 2 * lru_w + d

    w_main, w_gates = split_w_in(w_in, g0, g1)
    zero_delta = jnp.zeros((n_p, 2, heads, dk, LANES), F32)
    zero_lru = jnp.zeros((n_p, 2, lru_w), F32)
    new_delta, new_lru = [], []
    for l in range(depth):
        h, gates = norm1_gates(x, norm1_g[l], mod, l, w_gates, grp)
        proj = matmul(h, w_main, l, name="in_proj")
        gcol, gt = dn_gate_prep(gates, dn_a_log[l], dn_dt_bias[l], heads)

        dn_kw = dict(heads=heads, dk=dk, cols=(col_q, col_k, col_v, col_z))
        dn_p, s_delta = deltanet(proj, gcol, gt, dn_conv_w[l], dn_norm_g[l], zero_delta,
                                 n_seq=n_p, l=l_p, row0=0, **dn_kw)
        dn_s, _ = deltanet(proj, gcol, gt, dn_conv_w[l], dn_norm_g[l], state_delta[:, l],
                           n_seq=n_s, l=l_s, row0=grp.t_p, **dn_kw)
        lru_args = (proj, lru_conv_w[l], lru_conv_b[l], lru_wa[l], lru_wx[l], lru_ba[l], lru_bx[l],
                    lru_lambda[l])
        lru_p, s_lru = rglru(*lru_args, zero_lru, n_seq=n_p, l=l_p, row0=0, cols=(col_lx, col_ly))
        lru_s, _ = rglru(*lru_args, state_lru[:, l], n_seq=n_s, l=l_s, row0=grp.t_p, cols=(col_lx, col_ly))
        new_delta.append(s_delta)
        new_lru.append(s_lru)

        merged = branch_merge(dn_p, dn_s, lru_p, lru_s, w_branch_a[l].astype(BF16), w_branch_b[l].astype(BF16),
                              proj, (col_ga, col_gb), grp)
        x = out_proj_residual(merged, w_out[l].astype(BF16), x, mod, l, grp)

        h2, h2b, top_idx, top_w = norm2_router(x, norm2_g[l], mod, l, router_w[l].T, router_bias[l], grp)
        meta = route_metadata(top_idx[:TOP_K], n_exp, MOE_TM)
        hmid = moe_stage1(h2, w_exp_gate, w_exp_up, l, meta)
        slots = moe_stage2(hmid, w_exp_down, l, meta, TOP_K * t + 16)
        hs = shared_in(h2b, w_sh_gate[l].astype(BF16), w_sh_up[l].astype(BF16))
        x = moe_out_residual(hs, w_sh_down[l].astype(BF16), x, mod, l, slots, top_w.T, grp)

    y_prompt, y_sample = final_norm(x, final_norm_g, grp)
    return (y_prompt.reshape(n_p, l_p, d), y_sample.reshape(n_s, l_s, d),
            jnp.stack(new_delta, axis=1), jnp.stack(new_lru, axis=1))
```
